```python
import jax, jax.numpy as jnp
from jax import lax
import numpy as np

D_MODEL = 2048
BATCH = 4
SEQ = 8192
DEPTH = 2

GRID_W = 64
CTX_LEN = 256
N_MIXERS = 2
N_ADA = 6
RMS_EPS = 1e-6
N_HEADS = 16
Q_LORA = 512
KV_LORA = 512
QK_NOPE = 128
QK_ROPE = 64
V_DIM = 128
ROPE_AXIS = QK_ROPE // 2
ROPE_THETA = 10000.0
Q_BLOCK = 128
ATTN_SCALE = (QK_NOPE + QK_ROPE) ** -0.5
CONV_WIDTH = 3
N_EXPERTS = 32
TOP_K = 4
D_EXPERT = D_MODEL
SWIGLU_LIMIT = 7.0
SWIGLU_ALPHA = 1.702
EXPERT_BLOCK = 256

kernel_name = "hybrid_mla_shortconv_moe_dit"


def rms_norm(x, g):
    xf = x.astype(jnp.float32)
    y = xf * lax.rsqrt(jnp.mean(xf * xf, axis=-1, keepdims=True) + RMS_EPS)
    return (y * g.astype(jnp.float32)).astype(x.dtype)


def modulate(x, g, shift, scale):
    return rms_norm(x, g) * (1 + scale) + shift


def ada_chunks(cvec, w, b, n):
    d = w.shape[0]
    m = jax.nn.silu(cvec) @ w[:, : n * d] + b[: n * d]
    return jnp.split(m, n, axis=-1)


def grid_rope_tables(length):
    rows = length // GRID_W
    row = jnp.repeat(jnp.arange(rows), GRID_W).astype(jnp.float32)
    col = jnp.tile(jnp.arange(GRID_W), rows).astype(jnp.float32)
    inv = 1.0 / (ROPE_THETA ** (jnp.arange(0, ROPE_AXIS, 2, dtype=jnp.float32) / ROPE_AXIS))
    ang_r = row[:, None] * inv[None, :]
    ang_c = col[:, None] * inv[None, :]
    return jnp.cos(ang_r), jnp.sin(ang_r), jnp.cos(ang_c), jnp.sin(ang_c)


def _rotate(x, cos, sin):
    x1, x2 = jnp.split(x, 2, axis=-1)
    return jnp.concatenate([x1 * cos - x2 * sin, x2 * cos + x1 * sin], axis=-1)


def apply_rope2d(x, tables):
    cos_r, sin_r, cos_c, sin_c = tables

    def bc(t):
        return t.reshape((1, t.shape[0]) + (1,) * (x.ndim - 3) + (t.shape[1],)).astype(x.dtype)

    x_r, x_c = jnp.split(x, 2, axis=-1)
    return jnp.concatenate([_rotate(x_r, bc(cos_r), bc(sin_r)), _rotate(x_c, bc(cos_c), bc(sin_c))], axis=-1)


def mla_q(cq, g_q, w_q_up):
    b, l, _ = cq.shape
    q = (rms_norm(cq, g_q) @ w_q_up).reshape(b, l, N_HEADS, QK_NOPE + QK_ROPE)
    return q[..., :QK_NOPE], q[..., QK_NOPE:]


def mla_kv(ckv, g_kv, w_kv_up):
    b, l, _ = ckv.shape
    kv = (rms_norm(ckv, g_kv) @ w_kv_up).reshape(b, l, N_HEADS, QK_NOPE + V_DIM)
    return kv[..., :QK_NOPE], kv[..., QK_NOPE:]


def attend(q_nope, q_rope, k_nope, k_rope, v):
    b, l, h, _ = q_nope.shape
    nb = l // Q_BLOCK
    qn = q_nope.reshape(b, nb, Q_BLOCK, h, QK_NOPE).transpose(1, 0, 2, 3, 4)
    qr = q_rope.reshape(b, nb, Q_BLOCK, h, QK_ROPE).transpose(1, 0, 2, 3, 4)

    def one_block(args):
        qn_b, qr_b = args
        s = (jnp.einsum('bqhd,bkhd->bhqk', qn_b, k_nope).astype(jnp.float32)
             + jnp.einsum('bqhr,bkr->bhqk', qr_b, k_rope).astype(jnp.float32)) * ATTN_SCALE
        p = jax.nn.softmax(s, axis=-1).astype(v.dtype)
        return jnp.einsum('bhqk,bkhd->bqhd', p, v)

    o = lax.map(one_block, (qn, qr))
    return o.transpose(1, 0, 2, 3, 4).reshape(b, l, h * V_DIM)


def conv3_centred(z, w):
    zp = jnp.pad(z, ((0, 0), (1, 1), (0, 0)))
    return w[0] * zp[:, :-2] + w[1] * zp[:, 1:-1] + w[2] * zp[:, 2:]


def short_gated_conv(h, w_in, w_conv, w_out):
    gate_b, gate_c, u = jnp.split(h @ w_in, 3, axis=-1)
    return (gate_b * conv3_centred(gate_c * u, w_conv)) @ w_out


def expert_ffn(h, w_r, b_r, w_gu, b_gu, w_down, b_down):
    shp = h.shape
    t = h.reshape(-1, shp[-1])
    n_tok = t.shape[0]
    n_assign = n_tok * TOP_K
    logits = (t @ w_r + b_r).astype(jnp.float32)
    top_val, top_idx = lax.top_k(logits, TOP_K)
    gate = jax.nn.softmax(top_val, axis=-1).astype(h.dtype)
    flat_e = top_idx.reshape(-1)
    flat_tok = jnp.arange(n_assign, dtype=jnp.int32) // TOP_K
    flat_w = gate.reshape(-1)
    counts = jnp.bincount(flat_e, length=N_EXPERTS)
    padded = (counts + EXPERT_BLOCK - 1) // EXPERT_BLOCK * EXPERT_BLOCK
    padded_end = jnp.cumsum(padded)
    padded_start = padded_end - padded
    group_start = jnp.cumsum(counts) - counts
    order = jnp.argsort(flat_e)
    sorted_e = flat_e[order]
    dest = padded_start[sorted_e] + jnp.arange(n_assign, dtype=jnp.int32) - group_start[sorted_e]
    n_blocks = -(-(n_assign + N_EXPERTS * (EXPERT_BLOCK - 1)) // EXPERT_BLOCK)
    n_slots = n_blocks * EXPERT_BLOCK
    slot_tok = jnp.zeros((n_slots,), jnp.int32).at[dest].set(flat_tok[order])
    slot_w = jnp.zeros((n_slots,), h.dtype).at[dest].set(flat_w[order])
    block_start = jnp.arange(n_blocks, dtype=jnp.int32) * EXPERT_BLOCK
    block_e = jnp.minimum(jnp.sum(block_start[:, None] >= padded_end[None, :], axis=1), N_EXPERTS - 1)

    def run_block(args):
        tok, wgt, e = args
        xb = t[tok]
        gu = xb @ w_gu[e] + b_gu[e]
        glu, lin = jnp.split(gu, 2, axis=-1)
        glu = jnp.minimum(glu, SWIGLU_LIMIT)
        lin = jnp.clip(lin, -SWIGLU_LIMIT, SWIGLU_LIMIT)
        act = glu * jax.nn.sigmoid(SWIGLU_ALPHA * glu) * (lin + 1)
        return (act @ w_down[e] + b_down[e]) * wgt[:, None]

    y = lax.map(run_block, (slot_tok.reshape(n_blocks, EXPERT_BLOCK),
                            slot_w.reshape(n_blocks, EXPERT_BLOCK), block_e))
    out = jax.ops.segment_sum(y.reshape(n_slots, shp[-1]), slot_tok, num_segments=n_tok)
    return out.reshape(shp)


def setup_inputs(seed: int = 0) -> dict:
    key = jax.random.key(seed)
    ks = jax.random.split(key, 24)
    D, F = D_MODEL, D_EXPERT
    n_mla = (DEPTH + N_MIXERS - 1) // N_MIXERS
    n_conv = DEPTH // N_MIXERS

    def nrm(k, shape, scale):
        return jax.random.normal(k, shape, jnp.float32) * scale

    return {
        "x": nrm(ks[0], (BATCH, SEQ, D), 1.0),
        "c": nrm(ks[1], (BATCH, D), 1.0),
        "ctx": nrm(ks[2], (BATCH, CTX_LEN, D), 1.0),
        "c_ctx": nrm(ks[3], (D,), 1.0),
        "ada_w": nrm(ks[4], (DEPTH, D, N_ADA * D), 0.5 * D ** -0.5),
        "ada_b": nrm(ks[5], (DEPTH, N_ADA * D), 0.02),
        "norm_mix_g": 1.0 + nrm(ks[6], (DEPTH, D), 0.02),
        "norm_ffn_g": 1.0 + nrm(ks[7], (DEPTH, D), 0.02),
        "mla_w_in": nrm(ks[8], (n_mla, D, Q_LORA + KV_LORA + QK_ROPE), D ** -0.5),
        "mla_q_norm_g": 1.0 + nrm(ks[9], (n_mla, Q_LORA), 0.02),
        "mla_kv_norm_g": 1.0 + nrm(ks[10], (n_mla, KV_LORA), 0.02),
        "mla_w_q_up": nrm(ks[11], (n_mla, Q_LORA, N_HEADS * (QK_NOPE + QK_ROPE)), Q_LORA ** -0.5),
        "mla_w_kv_up": nrm(ks[12], (n_mla, KV_LORA, N_HEADS * (QK_NOPE + V_DIM)), KV_LORA ** -0.5),
        "mla_w_out": nrm(ks[13], (n_mla, N_HEADS * V_DIM, D), (N_HEADS * V_DIM) ** -0.5),
        "conv_w_in": nrm(ks[14], (n_conv, D, 3 * D), D ** -0.5),
        "conv_w": nrm(ks[15], (n_conv, CONV_WIDTH, D), CONV_WIDTH ** -0.5),
        "conv_w_out": nrm(ks[16], (n_conv, D, D), D ** -0.5),
        "router_w": nrm(ks[17], (DEPTH, D, N_EXPERTS), D ** -0.5),
        "router_b": nrm(ks[18], (DEPTH, N_EXPERTS), 0.01),
        "expert_w_gu": nrm(ks[19], (DEPTH, N_EXPERTS, D, 2 * F), D ** -0.5),
        "expert_b_gu": nrm(ks[20], (DEPTH, N_EXPERTS, 2 * F), 0.02),
        "expert_w_down": nrm(ks[21], (DEPTH, N_EXPERTS, F, D), F ** -0.5),
        "expert_b_down": nrm(ks[22], (DEPTH, N_EXPERTS, D), 0.02),
        "final_norm_g": 1.0 + nrm(ks[23], (D,), 0.02),
    }


def reference(x, c, ctx, c_ctx, ada_w, ada_b, norm_mix_g, norm_ffn_g, mla_w_in, mla_q_norm_g,
              mla_kv_norm_g, mla_w_q_up, mla_w_kv_up, mla_w_out, conv_w_in, conv_w, conv_w_out,
              router_w, router_b, expert_w_gu, expert_b_gu, expert_w_down, expert_b_down,
              final_norm_g):
    seq_len = x.shape[1]
    rope_tabs = grid_rope_tables(seq_len)
    c_lat = c[:, None, :]
    c_con = c_ctx[None, None, :]
    h_ctx = ctx

    for i in range(DEPTH):
        kind = i % N_MIXERS
        j = i // N_MIXERS
        update_ctx = any(k % N_MIXERS == 0 for k in range(i + 1, DEPTH))
        ctx_read = (kind == 0) or update_ctx

        sh1, sc1, g1, sh2, sc2, g2 = ada_chunks(c_lat, ada_w[i], ada_b[i], N_ADA)
        hx = modulate(x, norm_mix_g[i], sh1, sc1)
        if ctx_read:
            cmod = ada_chunks(c_con, ada_w[i], ada_b[i], N_ADA if update_ctx else 2)
            hc = modulate(h_ctx, norm_mix_g[i], cmod[0], cmod[1])

        if kind == 0:
            w_in = mla_w_in[j]
            if update_ctx:
                cq_c, ckv_c, kr_c = jnp.split(hc @ w_in, [Q_LORA, Q_LORA + KV_LORA], axis=-1)
            else:
                ckv_c, kr_c = jnp.split(hc @ w_in[:, Q_LORA:], [KV_LORA], axis=-1)
            k_nope_c, v_c = mla_kv(ckv_c, mla_kv_norm_g[j], mla_w_kv_up[j])
            cq, ckv, kr = jnp.split(hx @ w_in, [Q_LORA, Q_LORA + KV_LORA], axis=-1)
            q_nope, q_rope = mla_q(cq, mla_q_norm_g[j], mla_w_q_up[j])
            q_rope = apply_rope2d(q_rope, rope_tabs)
            kr = apply_rope2d(kr, rope_tabs)
            k_nope, v = mla_kv(ckv, mla_kv_norm_g[j], mla_w_kv_up[j])
            o = attend(q_nope, q_rope,
                       jnp.concatenate([k_nope_c, k_nope], axis=1),
                       jnp.concatenate([kr_c, kr], axis=1),
                       jnp.concatenate([v_c, v], axis=1))
            mix = o @ mla_w_out[j]
            if update_ctx:
                qn_c, qr_c = mla_q(cq_c, mla_q_norm_g[j], mla_w_q_up[j])
                mix_c = attend(qn_c, qr_c, k_nope_c, kr_c, v_c) @ mla_w_out[j]
        else:
            mix = short_gated_conv(hx, conv_w_in[j], conv_w[j], conv_w_out[j])
            if update_ctx:
                mix_c = short_gated_conv(hc, conv_w_in[j], conv_w[j], conv_w_out[j])

        x = x + g1 * mix
        x = x + g2 * expert_ffn(modulate(x, norm_ffn_g[i], sh2, sc2), router_w[i], router_b[i],
                                expert_w_gu[i], expert_b_gu[i], expert_w_down[i], expert_b_down[i])
        if update_ctx:
            h_ctx = h_ctx + cmod[2] * mix_c
            h_ctx = h_ctx + cmod[5] * expert_ffn(
                modulate(h_ctx, norm_ffn_g[i], cmod[3], cmod[4]), router_w[i], router_b[i],
                expert_w_gu[i], expert_b_gu[i], expert_w_down[i], expert_b_down[i])

    return rms_norm(x, final_norm_g)
```

```python
import functools

import jax
import jax.numpy as jnp
import numpy as np
from jax import lax
from jax.experimental import pallas as pl
from jax.experimental.pallas import tpu as pltpu

GRID_W = 64
RMS_EPS = 1e-6
N_ADA = 6
N_HEADS = 16
Q_LORA = 512
KV_LORA = 512
QK_NOPE = 128
QK_ROPE = 64
V_DIM = 128
ROPE_AXIS = QK_ROPE // 2
ROPE_THETA = 10000.0
QK_DIM = QK_NOPE + QK_ROPE
ATTN_SCALE = QK_DIM ** -0.5
N_EXPERTS = 32
TOP_K = 4
SWIGLU_LIMIT = 7.0
SWIGLU_ALPHA = 1.702

LANES = 128
SUBLANES = 8
VMEM_LIMIT = 56 * 1024 * 1024

ADA_ROWS = 8
EXPERT_TM = 512
GATHER_ROWS = 2048
COMBINE_TOK = 128

BF16 = jnp.bfloat16
F32 = jnp.float32


def _cparams(*sem):
    return pltpu.CompilerParams(dimension_semantics=sem, vmem_limit_bytes=VMEM_LIMIT)


def _rms(x):
    return x * lax.rsqrt(jnp.mean(x * x, axis=-1, keepdims=True) + RMS_EPS)


def _modulate(x, g, shift, scale):
    return _rms(x) * g * (1.0 + scale) + shift


def _ada_kernel(cv_ref, w_ref, b_ref, o_ref):
    cv = cv_ref[...]
    s = cv * (1.0 / (1.0 + jnp.exp(-cv)))
    o_ref[...] = jnp.dot(s, w_ref[...], preferred_element_type=F32,
                         precision=lax.Precision.HIGHEST) + b_ref[...]


def _ada(cv, ada_w, ada_b3, layer):
    d = cv.shape[1]
    n = ada_w.shape[2]
    tn = 1024
    return pl.pallas_call(
        _ada_kernel,
        out_shape=jax.ShapeDtypeStruct((ADA_ROWS, n), F32),
        grid=(n // tn,),
        in_specs=[
            pl.BlockSpec((ADA_ROWS, d), lambda j: (0, 0)),
            pl.BlockSpec((None, d, tn), lambda j: (layer, 0, j)),
            pl.BlockSpec((None, 1, tn), lambda j: (layer, 0, j)),
        ],
        out_specs=pl.BlockSpec((ADA_ROWS, tn), lambda j: (0, j)),
        compiler_params=_cparams("arbitrary"),
        name="ada",
    )(cv, ada_w, ada_b3)


def _mla_in_kernel(x_ref, g_ref, sh_ref, sc_ref, w_ref, gq_ref, gkv_ref, cos_ref, sin_ref,
                   cq_ref, ckv_ref, kr_ref):
    h = _modulate(x_ref[0], g_ref[...], sh_ref[0], sc_ref[0]).astype(BF16)
    p = jnp.dot(h, w_ref[...], preferred_element_type=F32)
    cq_ref[0] = (_rms(p[:, :Q_LORA]) * gq_ref[...]).astype(BF16)
    ckv_ref[0] = (_rms(p[:, Q_LORA:Q_LORA + KV_LORA]) * gkv_ref[...]).astype(BF16)
    o = Q_LORA + KV_LORA
    kr = p[:, o:o + QK_ROPE]
    kr_rot = p[:, o + QK_ROPE:o + 2 * QK_ROPE]
    kr_ref[0] = (kr * cos_ref[...] + kr_rot * sin_ref[...]).astype(BF16)


def _mla_in_ctx_kernel(x_ref, g_ref, sh_ref, sc_ref, w_ref, gkv_ref, ckv_ref, kr_ref):
    h = _modulate(x_ref[0], g_ref[...], sh_ref[0], sc_ref[0]).astype(BF16)
    p = jnp.dot(h, w_ref[...], preferred_element_type=F32)
    ckv_ref[0] = (_rms(p[:, :KV_LORA]) * gkv_ref[...]).astype(BF16)
    kr_ref[0] = p[:, KV_LORA:KV_LORA + QK_ROPE].astype(BF16)


def _mla_in(x, g, mod3, layer_chunk0, w_all, gq, gkv, cos, sin):
    b, l, d = x.shape
    tm = 512
    nw = w_all.shape[1]
    row = lambda bi, i: (bi, i, 0)
    return pl.pallas_call(
        _mla_in_kernel,
        out_shape=(jax.ShapeDtypeStruct((b, l, Q_LORA), BF16),
                   jax.ShapeDtypeStruct((b, l, KV_LORA), BF16),
                   jax.ShapeDtypeStruct((b, l, QK_ROPE), BF16)),
        grid=(b, l // tm),
        in_specs=[
            pl.BlockSpec((1, tm, d), row),
            pl.BlockSpec((1, d), lambda bi, i: (0, 0)),
            pl.BlockSpec((1, 1, d), lambda bi, i: (bi, 0, layer_chunk0)),
            pl.BlockSpec((1, 1, d), lambda bi, i: (bi, 0, layer_chunk0 + 1)),
            pl.BlockSpec((d, nw), lambda bi, i: (0, 0)),
            pl.BlockSpec((1, Q_LORA), lambda bi, i: (0, 0)),
            pl.BlockSpec((1, KV_LORA), lambda bi, i: (0, 0)),
            pl.BlockSpec((tm, QK_ROPE), lambda bi, i: (i, 0)),
            pl.BlockSpec((tm, QK_ROPE), lambda bi, i: (i, 0)),
        ],
        out_specs=(pl.BlockSpec((1, tm, Q_LORA), row),
                   pl.BlockSpec((1, tm, KV_LORA), row),
                   pl.BlockSpec((1, tm, QK_ROPE), row)),
        compiler_params=_cparams("arbitrary", "arbitrary"),
        name="mla_in",
    )(x, g, mod3, mod3, w_all, gq, gkv, cos, sin)


def _mla_in_ctx(ctx, g, mod3, ctx_row, w_kv, gkv):
    b, l, d = ctx.shape
    tm = l
    nw = w_kv.shape[1]
    row = lambda bi, i: (bi, i, 0)
    return pl.pallas_call(
        _mla_in_ctx_kernel,
        out_shape=(jax.ShapeDtypeStruct((b, l, KV_LORA), BF16),
                   jax.ShapeDtypeStruct((b, l, QK_ROPE), BF16)),
        grid=(b, l // tm),
        in_specs=[
            pl.BlockSpec((1, tm, d), row),
            pl.BlockSpec((1, d), lambda bi, i: (0, 0)),
            pl.BlockSpec((1, 1, d), lambda bi, i: (ctx_row, 0, 0)),
            pl.BlockSpec((1, 1, d), lambda bi, i: (ctx_row, 0, 1)),
            pl.BlockSpec((d, nw), lambda bi, i: (0, 0)),
            pl.BlockSpec((1, KV_LORA), lambda bi, i: (0, 0)),
        ],
        out_specs=(pl.BlockSpec((1, tm, KV_LORA), row),
                   pl.BlockSpec((1, tm, QK_ROPE), row)),
        compiler_params=_cparams("arbitrary", "arbitrary"),
        name="mla_in_ctx",
    )(ctx, g, mod3, mod3, w_kv, gkv)


Q_HEAD_COLS = QK_NOPE + 2 * QK_ROPE
KV_HEAD_COLS = QK_NOPE + V_DIM


def _q_up_kernel(cq_ref, w_ref, cos_ref, sin_ref, q_ref):
    cq = cq_ref[0]
    cos = cos_ref[...]
    sin = sin_ref[...]
    for h in range(N_HEADS):
        p = jnp.dot(cq, w_ref[:, h * Q_HEAD_COLS:(h + 1) * Q_HEAD_COLS],
                    preferred_element_type=F32)
        rope = p[:, QK_NOPE:QK_NOPE + QK_ROPE] * cos + p[:, QK_NOPE + QK_ROPE:] * sin
        q_ref[0, h, :, :QK_NOPE] = (p[:, :QK_NOPE] * ATTN_SCALE).astype(BF16)
        q_ref[0, h, :, QK_NOPE:] = (rope * ATTN_SCALE).astype(BF16)


def _q_up(cq, w_q, cos, sin):
    b, l, _ = cq.shape
    tm = 512
    return pl.pallas_call(
        _q_up_kernel,
        out_shape=jax.ShapeDtypeStruct((b, N_HEADS, l, QK_DIM), BF16),
        grid=(b, l // tm),
        in_specs=[
            pl.BlockSpec((1, tm, Q_LORA), lambda bi, i: (bi, i, 0)),
            pl.BlockSpec(w_q.shape, lambda bi, i: (0, 0)),
            pl.BlockSpec((tm, QK_ROPE), lambda bi, i: (i, 0)),
            pl.BlockSpec((tm, QK_ROPE), lambda bi, i: (i, 0)),
        ],
        out_specs=pl.BlockSpec((1, N_HEADS, tm, QK_DIM), lambda bi, i: (bi, 0, i, 0)),
        compiler_params=_cparams("arbitrary", "arbitrary"),
        name="q_up",
    )(cq, w_q, cos, sin)


def _kv_up_kernel(ckv_ref, kr_ref, w_ref, k_ref, v_ref):
    ckv = ckv_ref[0]
    kr = kr_ref[0]
    for h in range(N_HEADS):
        p = jnp.dot(ckv, w_ref[:, h * KV_HEAD_COLS:(h + 1) * KV_HEAD_COLS],
                    preferred_element_type=F32)
        k_ref[0, h, :, :QK_NOPE] = p[:, :QK_NOPE].astype(BF16)
        k_ref[0, h, :, QK_NOPE:] = kr
        v_ref[0, h] = p[:, QK_NOPE:].astype(BF16)


def _kv_up(ckv, kr, w_kv, tm):
    b, l, _ = ckv.shape
    return pl.pallas_call(
        _kv_up_kernel,
        out_shape=(jax.ShapeDtypeStruct((b, N_HEADS, l, QK_DIM), BF16),
                   jax.ShapeDtypeStruct((b, N_HEADS, l, V_DIM), BF16)),
        grid=(b, l // tm),
        in_specs=[
            pl.BlockSpec((1, tm, KV_LORA), lambda bi, i: (bi, i, 0)),
            pl.BlockSpec((1, tm, QK_ROPE), lambda bi, i: (bi, i, 0)),
            pl.BlockSpec(w_kv.shape, lambda bi, i: (0, 0)),
        ],
        out_specs=(pl.BlockSpec((1, N_HEADS, tm, QK_DIM), lambda bi, i: (bi, 0, i, 0)),
                   pl.BlockSpec((1, N_HEADS, tm, V_DIM), lambda bi, i: (bi, 0, i, 0))),
        compiler_params=_cparams("arbitrary", "arbitrary"),
        name="kv_up",
    )(ckv, kr, w_kv)


def _attn_kernel(q_ref, k_ref, v_ref, o_ref, *, tk, n_chunks):
    q = q_ref[0, 0]
    tq = q.shape[0]

    def body(c, carry):
        m, l, acc = carry
        start = pl.multiple_of(c * tk, tk)
        k = k_ref[0, 0, pl.ds(start, tk), :]
        v = v_ref[0, 0, pl.ds(start, tk), :]
        s = lax.dot_general(q, k, (((1,), (1,)), ((), ())), preferred_element_type=F32)
        m_new = jnp.maximum(m, jnp.max(s, axis=-1, keepdims=True))
        alpha = jnp.exp(m - m_new)
        p = jnp.exp(s - m_new)
        l = alpha * l + jnp.sum(p, axis=-1, keepdims=True)
        acc = alpha * acc + jnp.dot(p.astype(BF16), v, preferred_element_type=F32)
        return m_new, l, acc

    init = (jnp.full((tq, 1), -jnp.inf, F32), jnp.zeros((tq, 1), F32),
            jnp.zeros((tq, V_DIM), F32))
    _, l, acc = lax.fori_loop(0, n_chunks, body, init)
    o_ref[0] = (acc / l).astype(BF16)


def _attention(q, k, v, tq, tk):
    b, h, l, _ = q.shape
    lk = k.shape[2]
    return pl.pallas_call(
        functools.partial(_attn_kernel, tk=tk, n_chunks=lk // tk),
        out_shape=jax.ShapeDtypeStruct((b, l, h * V_DIM), BF16),
        grid=(b, h, l // tq),
        in_specs=[
            pl.BlockSpec((1, 1, tq, QK_DIM), lambda bi, hi, i: (bi, hi, i, 0)),
            pl.BlockSpec((1, 1, lk, QK_DIM), lambda bi, hi, i: (bi, hi, 0, 0)),
            pl.BlockSpec((1, 1, lk, V_DIM), lambda bi, hi, i: (bi, hi, 0, 0)),
        ],
        out_specs=pl.BlockSpec((1, tq, V_DIM), lambda bi, hi, i: (bi, i, hi)),
        compiler_params=_cparams("arbitrary", "arbitrary", "arbitrary"),
        name="attention",
    )(q, k, v)


def _proj_res_kernel(o_ref, w_ref, x_ref, g1_ref, out_ref):
    mix = jnp.dot(o_ref[0], w_ref[...], preferred_element_type=F32)
    out_ref[0] = x_ref[0] + g1_ref[0] * mix


def _proj_res(o, w_out, x, mod3, gate_chunk):
    b, l, d = x.shape
    tm = 512
    row = lambda bi, i: (bi, i, 0)
    return pl.pallas_call(
        _proj_res_kernel,
        out_shape=jax.ShapeDtypeStruct((b, l, d), F32),
        grid=(b, l // tm),
        in_specs=[
            pl.BlockSpec((1, tm, o.shape[2]), row),
            pl.BlockSpec(w_out.shape, lambda bi, i: (0, 0)),
            pl.BlockSpec((1, tm, d), row),
            pl.BlockSpec((1, 1, d), lambda bi, i: (bi, 0, gate_chunk)),
        ],
        out_specs=pl.BlockSpec((1, tm, d), row),
        compiler_params=_cparams("arbitrary", "arbitrary"),
        name="proj_res",
    )(o, w_out, x, mod3)


def _conv_in_kernel(x_ref, g_ref, sh_ref, sc_ref, wb_ref, wc_ref, wu_ref, gb_ref, z_ref, h_ref):
    @pl.when(pl.program_id(2) == 0)
    def _():
        h_ref[...] = _modulate(x_ref[0], g_ref[...], sh_ref[0], sc_ref[0]).astype(BF16)

    h = h_ref[...]
    gb_ref[0] = jnp.dot(h, wb_ref[...], preferred_element_type=F32).astype(BF16)
    gc = jnp.dot(h, wc_ref[...], preferred_element_type=F32)
    u = jnp.dot(h, wu_ref[...], preferred_element_type=F32)
    z_ref[0] = (gc * u).astype(BF16)


def _conv_in(x, g, mod3, chunk0, w_in):
    b, l, d = x.shape
    tm, tn = 512, 512
    nn = d // tn
    row = lambda bi, i, j: (bi, i, 0)
    col = lambda bi, i, j: (bi, i, j)
    return pl.pallas_call(
        _conv_in_kernel,
        out_shape=(jax.ShapeDtypeStruct((b, l, d), BF16), jax.ShapeDtypeStruct((b, l, d), BF16)),
        grid=(b, l // tm, nn),
        in_specs=[
            pl.BlockSpec((1, tm, d), row),
            pl.BlockSpec((1, d), lambda bi, i, j: (0, 0)),
            pl.BlockSpec((1, 1, d), lambda bi, i, j: (bi, 0, chunk0)),
            pl.BlockSpec((1, 1, d), lambda bi, i, j: (bi, 0, chunk0 + 1)),
            pl.BlockSpec((d, tn), lambda bi, i, j: (0, j)),
            pl.BlockSpec((d, tn), lambda bi, i, j: (0, nn + j)),
            pl.BlockSpec((d, tn), lambda bi, i, j: (0, 2 * nn + j)),
        ],
        out_specs=(pl.BlockSpec((1, tm, tn), col), pl.BlockSpec((1, tm, tn), col)),
        scratch_shapes=[pltpu.VMEM((tm, d), BF16)],
        compiler_params=_cparams("arbitrary", "arbitrary", "arbitrary"),
        name="conv_in",
    )(x, g, mod3, mod3, w_in, w_in, w_in)


def _conv_out_kernel(z_ref, zp_ref, zn_ref, gb_ref, cw_ref, w_ref, x_ref, g1_ref, out_ref):
    i = pl.program_id(1)
    last = pl.num_programs(1) - 1
    z = z_ref[0].astype(F32)
    tm = z.shape[0]
    prev_row = jnp.where(i > 0, zp_ref[0, SUBLANES - 1:SUBLANES, :].astype(F32), 0.0)
    next_row = jnp.where(i < last, zn_ref[0, 0:1, :].astype(F32), 0.0)
    ridx = lax.broadcasted_iota(jnp.int32, z.shape, 0)
    z_prev = jnp.where(ridx == 0, prev_row, pltpu.roll(z, 1, 0))
    z_next = jnp.where(ridx == tm - 1, next_row, pltpu.roll(z, tm - 1, 0))
    conv = cw_ref[0:1, :] * z_prev + cw_ref[1:2, :] * z + cw_ref[2:3, :] * z_next
    y = (gb_ref[0].astype(F32) * conv).astype(BF16)
    mix = jnp.dot(y, w_ref[...], preferred_element_type=F32)
    out_ref[0] = x_ref[0] + g1_ref[0] * mix


def _conv_out(z, gb, conv_w, w_out, x, mod3, gate_chunk):
    b, l, d = x.shape
    tm = 256
    hb = tm // SUBLANES
    n_halo = l // SUBLANES
    row = lambda bi, i: (bi, i, 0)
    return pl.pallas_call(
        _conv_out_kernel,
        out_shape=jax.ShapeDtypeStruct((b, l, d), F32),
        grid=(b, l // tm),
        in_specs=[
            pl.BlockSpec((1, tm, d), row),
            pl.BlockSpec((1, SUBLANES, d), lambda bi, i: (bi, jnp.maximum(i * hb - 1, 0), 0)),
            pl.BlockSpec((1, SUBLANES, d),
                         lambda bi, i: (bi, jnp.minimum((i + 1) * hb, n_halo - 1), 0)),
            pl.BlockSpec((1, tm, d), row),
            pl.BlockSpec(conv_w.shape, lambda bi, i: (0, 0)),
            pl.BlockSpec(w_out.shape, lambda bi, i: (0, 0)),
            pl.BlockSpec((1, tm, d), row),
            pl.BlockSpec((1, 1, d), lambda bi, i: (bi, 0, gate_chunk)),
        ],
        out_specs=pl.BlockSpec((1, tm, d), row),
        compiler_params=_cparams("arbitrary", "arbitrary"),
        name="conv_out",
    )(z, z, z, gb, conv_w, w_out, x, mod3)


def _router_kernel(x_ref, g_ref, sh_ref, sc_ref, wr_ref, br_ref, h_ref, idx_ref, gate_ref):
    h = _modulate(x_ref[0], g_ref[...], sh_ref[0], sc_ref[0])
    h_ref[...] = h
    logits = jnp.dot(h, wr_ref[...], preferred_element_type=F32,
                     precision=lax.Precision.HIGHEST) + br_ref[...]
    lane = lax.broadcasted_iota(jnp.int32, logits.shape, 1).astype(F32)
    work = jnp.where(lane < N_EXPERTS, logits, -jnp.inf)
    vals, idxs = [], []
    for _ in range(TOP_K):
        mx = jnp.max(work, axis=-1, keepdims=True)
        ix = jnp.min(jnp.where(work == mx, lane, float(LANES)), axis=-1, keepdims=True)
        vals.append(mx)
        idxs.append(ix.astype(jnp.int32))
        work = jnp.where(lane == ix, -jnp.inf, work)
    ex = [jnp.exp(v - vals[0]) for v in vals]
    den = ex[0] + ex[1] + ex[2] + ex[3]
    for k in range(TOP_K):
        idx_ref[:, k:k + 1] = idxs[k]
        gate_ref[:, k:k + 1] = ex[k] / den


def _router(x, g, mod3, chunk0, w_r, b_r):
    b, l, d = x.shape
    tm = 512
    nt = l // tm
    t = b * l
    return pl.pallas_call(
        _router_kernel,
        out_shape=(jax.ShapeDtypeStruct((t, d), F32),
                   jax.ShapeDtypeStruct((t, TOP_K), jnp.int32),
                   jax.ShapeDtypeStruct((t, TOP_K), F32)),
        grid=(b, nt),
        in_specs=[
            pl.BlockSpec((1, tm, d), lambda bi, i: (bi, i, 0)),
            pl.BlockSpec((1, d), lambda bi, i: (0, 0)),
            pl.BlockSpec((1, 1, d), lambda bi, i: (bi, 0, chunk0)),
            pl.BlockSpec((1, 1, d), lambda bi, i: (bi, 0, chunk0 + 1)),
            pl.BlockSpec((d, LANES), lambda bi, i: (0, 0)),
            pl.BlockSpec((1, LANES), lambda bi, i: (0, 0)),
        ],
        out_specs=(pl.BlockSpec((tm, d), lambda bi, i: (bi * nt + i, 0)),
                   pl.BlockSpec((tm, TOP_K), lambda bi, i: (bi * nt + i, 0)),
                   pl.BlockSpec((tm, TOP_K), lambda bi, i: (bi * nt + i, 0))),
        compiler_params=_cparams("arbitrary", "arbitrary"),
        name="router",
    )(x, g, mod3, mod3, w_r, b_r)


def _gather_kernel(tok_ref, h_ref, xs_ref, sem):
    base = pl.program_id(0) * GATHER_ROWS

    def issue(r, carry):
        tok = tok_ref[0, 0, r]
        pltpu.make_async_copy(h_ref.at[pl.ds(tok, 1)], xs_ref.at[pl.ds(base + r, 1)], sem).start()
        return carry

    lax.fori_loop(0, GATHER_ROWS, issue, 0)

    def drain(r, carry):
        pltpu.make_async_copy(h_ref.at[pl.ds(0, 1)], xs_ref.at[pl.ds(base + r, 1)], sem).wait()
        return carry

    lax.fori_loop(0, GATHER_ROWS, drain, 0)


def _gather_rows(slot_tok, h):
    n_slots = slot_tok.shape[0]
    d = h.shape[1]
    n_steps = n_slots // GATHER_ROWS
    tok3 = slot_tok.reshape(n_steps, 1, GATHER_ROWS)
    return pl.pallas_call(
        _gather_kernel,
        out_shape=jax.ShapeDtypeStruct((n_slots, d), h.dtype),
        grid=(n_steps,),
        in_specs=[
            pl.BlockSpec((1, 1, GATHER_ROWS), lambda i: (i, 0, 0), memory_space=pltpu.SMEM),
            pl.BlockSpec(memory_space=pl.ANY),
        ],
        out_specs=pl.BlockSpec(memory_space=pl.ANY),
        scratch_shapes=[pltpu.SemaphoreType.DMA],
        compiler_params=_cparams("arbitrary"),
        name="gather_rows",
    )(tok3, h)


def _expert_changed(be_ref):
    bi = pl.program_id(1)
    prev = be_ref[jnp.maximum(bi - 1, 0)]
    return jnp.logical_or(bi == 0, be_ref[bi] != prev)


def _gmm_gu_kernel(be_ref, x_ref, wg_ref, wl_ref, bg_ref, bl_ref, act_ref, wg_s, wl_s):
    @pl.when(_expert_changed(be_ref))
    def _():
        wg_s[...] = wg_ref[...].astype(BF16)
        wl_s[...] = wl_ref[...].astype(BF16)

    xb = x_ref[...].astype(BF16)
    glu = jnp.dot(xb, wg_s[...], preferred_element_type=F32) + bg_ref[...]
    lin = jnp.dot(xb, wl_s[...], preferred_element_type=F32) + bl_ref[...]
    glu = jnp.minimum(glu, SWIGLU_LIMIT)
    lin = jnp.clip(lin, -SWIGLU_LIMIT, SWIGLU_LIMIT)
    sig = 1.0 / (1.0 + jnp.exp(-SWIGLU_ALPHA * glu))
    act_ref[...] = (glu * sig * (lin + 1.0)).astype(BF16)


def _gmm_gu(block_e, xs, w_gu, b_gu4, layer):
    n_slots, d = xs.shape
    f = w_gu.shape[3] // 2
    tm, tn = EXPERT_TM, 512
    nn = f // tn
    return pl.pallas_call(
        _gmm_gu_kernel,
        out_shape=jax.ShapeDtypeStruct((n_slots, f), BF16),
        grid_spec=pltpu.PrefetchScalarGridSpec(
            num_scalar_prefetch=1,
            grid=(nn, n_slots // tm),
            in_specs=[
                pl.BlockSpec((tm, d), lambda j, i, be: (i, 0)),
                pl.BlockSpec((None, None, d, tn), lambda j, i, be: (layer, be[i], 0, j)),
                pl.BlockSpec((None, None, d, tn), lambda j, i, be: (layer, be[i], 0, nn + j)),
                pl.BlockSpec((None, None, 1, tn), lambda j, i, be: (layer, be[i], 0, j)),
                pl.BlockSpec((None, None, 1, tn), lambda j, i, be: (layer, be[i], 0, nn + j)),
            ],
            out_specs=pl.BlockSpec((tm, tn), lambda j, i, be: (i, j)),
            scratch_shapes=[pltpu.VMEM((d, tn), BF16), pltpu.VMEM((d, tn), BF16)],
        ),
        compiler_params=_cparams("arbitrary", "arbitrary"),
        name="gmm_gate_up",
    )(block_e, xs, w_gu, w_gu, b_gu4, b_gu4)


def _gmm_down_kernel(be_ref, a_ref, w_ref, b_ref, sw_ref, y_ref, w_s):
    @pl.when(_expert_changed(be_ref))
    def _():
        w_s[...] = w_ref[...].astype(BF16)

    y = jnp.dot(a_ref[...], w_s[...], preferred_element_type=F32) + b_ref[...]
    y_ref[...] = y * sw_ref[...]


def _gmm_down(block_e, act, w_down, b_down4, slot_w, layer):
    n_slots, f = act.shape
    d = w_down.shape[3]
    tm, tn = EXPERT_TM, 512
    return pl.pallas_call(
        _gmm_down_kernel,
        out_shape=jax.ShapeDtypeStruct((n_slots, d), F32),
        grid_spec=pltpu.PrefetchScalarGridSpec(
            num_scalar_prefetch=1,
            grid=(d // tn, n_slots // tm),
            in_specs=[
                pl.BlockSpec((tm, f), lambda j, i, be: (i, 0)),
                pl.BlockSpec((None, None, f, tn), lambda j, i, be: (layer, be[i], 0, j)),
                pl.BlockSpec((None, None, 1, tn), lambda j, i, be: (layer, be[i], 0, j)),
                pl.BlockSpec((tm, 1), lambda j, i, be: (i, 0)),
            ],
            out_specs=pl.BlockSpec((tm, tn), lambda j, i, be: (i, j)),
            scratch_shapes=[pltpu.VMEM((f, tn), BF16)],
        ),
        compiler_params=_cparams("arbitrary", "arbitrary"),
        name="gmm_down",
    )(block_e, act, w_down, b_down4, slot_w)


COMBINE_ROWS = COMBINE_TOK * TOP_K


def _combine_copy(ys_ref, buf_ref, sem, slot, row, pos):
    return pltpu.make_async_copy(ys_ref.at[pl.ds(pos, 1)], buf_ref.at[slot, pl.ds(row, 1)],
                                 sem.at[slot])


def _combine_kernel(pos_ref, posn_ref, ys_ref, x_ref, g2_ref, gf_ref, out_ref, buf_ref, sem,
                    *, final_norm):
    i = pl.program_id(0)
    n = pl.num_programs(0)
    slot = lax.rem(i, 2)

    def issue_from(p_ref, dst_slot):
        def issue(r, carry):
            _combine_copy(ys_ref, buf_ref, sem, dst_slot, r, p_ref[0, 0, r]).start()
            return carry
        lax.fori_loop(0, COMBINE_ROWS, issue, 0)

    @pl.when(i == 0)
    def _():
        issue_from(pos_ref, 0)

    @pl.when(i + 1 < n)
    def _():
        issue_from(posn_ref, 1 - slot)

    def drain(r, carry):
        _combine_copy(ys_ref, buf_ref, sem, slot, r, 0).wait()
        return carry

    lax.fori_loop(0, COMBINE_ROWS, drain, 0)

    y = buf_ref[slot, 0:COMBINE_TOK, :]
    for k in range(1, TOP_K):
        y = y + buf_ref[slot, k * COMBINE_TOK:(k + 1) * COMBINE_TOK, :]
    out = x_ref[...] + g2_ref[0] * y
    if final_norm:
        out = _rms(out) * gf_ref[...]
    out_ref[...] = out


def _combine(pos, ys, x2, mod3, gate_chunk, tokens_per_batch, final_g, final_norm):
    t, d = x2.shape
    n_steps = t // COMBINE_TOK
    steps_per_batch = tokens_per_batch // COMBINE_TOK
    pos3 = pos.reshape(n_steps, COMBINE_TOK, TOP_K).transpose(0, 2, 1).reshape(n_steps, 1, COMBINE_ROWS)
    return pl.pallas_call(
        functools.partial(_combine_kernel, final_norm=final_norm),
        out_shape=jax.ShapeDtypeStruct((t, d), F32),
        grid=(n_steps,),
        in_specs=[
            pl.BlockSpec((1, 1, COMBINE_ROWS), lambda i: (i, 0, 0), memory_space=pltpu.SMEM),
            pl.BlockSpec((1, 1, COMBINE_ROWS), lambda i: (jnp.minimum(i + 1, n_steps - 1), 0, 0),
                         memory_space=pltpu.SMEM),
            pl.BlockSpec(memory_space=pl.ANY),
            pl.BlockSpec((COMBINE_TOK, d), lambda i: (i, 0)),
            pl.BlockSpec((1, 1, d), lambda i: (i // steps_per_batch, 0, gate_chunk)),
            pl.BlockSpec((1, d), lambda i: (0, 0)),
        ],
        out_specs=pl.BlockSpec((COMBINE_TOK, d), lambda i: (i, 0)),
        scratch_shapes=[pltpu.VMEM((2, COMBINE_ROWS, d), F32), pltpu.SemaphoreType.DMA((2,))],
        compiler_params=_cparams("arbitrary"),
        name="combine",
    )(pos3, pos3, ys, x2, mod3, final_g)


def _slot_plan(top_idx, gate):
    t = top_idx.shape[0]
    n_assign = t * TOP_K
    flat_e = top_idx.reshape(-1)
    onehot = (flat_e[:, None] == jnp.arange(N_EXPERTS, dtype=jnp.int32)[None, :]).astype(jnp.int32)
    csum = jnp.cumsum(onehot, axis=0)
    counts = csum[-1]
    rank = jnp.sum(csum * onehot, axis=1) - 1
    padded = (counts + EXPERT_TM - 1) // EXPERT_TM * EXPERT_TM
    padded_end = jnp.cumsum(padded)
    padded_start = padded_end - padded
    pos = (padded_start[flat_e] + rank).astype(jnp.int32)
    n_blocks = -(-(n_assign + N_EXPERTS * (EXPERT_TM - 1)) // EXPERT_TM)
    n_blocks = -(-n_blocks * EXPERT_TM // GATHER_ROWS) * GATHER_ROWS // EXPERT_TM
    n_slots = n_blocks * EXPERT_TM
    flat_tok = jnp.arange(n_assign, dtype=jnp.int32) // TOP_K
    slot_tok = jnp.zeros((n_slots,), jnp.int32).at[pos].set(flat_tok)
    slot_w = jnp.zeros((n_slots,), F32).at[pos].set(gate.reshape(-1))
    block_start = jnp.arange(n_blocks, dtype=jnp.int32) * EXPERT_TM
    block_e = jnp.minimum(jnp.sum(block_start[:, None] >= padded_end[None, :], axis=1),
                          N_EXPERTS - 1).astype(jnp.int32)
    return pos, slot_tok, slot_w, block_e


def _expert_ffn_residual(x, mod3, layer, norm_g, w_r, b_r, w_gu, b_gu4, w_down, b_down4,
                         final_g, final_norm):
    b, l, d = x.shape
    h, top_idx, gate = _router(x, norm_g, mod3, 3, w_r, b_r)
    pos, slot_tok, slot_w, block_e = _slot_plan(top_idx, gate)
    xs = _gather_rows(slot_tok, h)
    act = _gmm_gu(block_e, xs, w_gu, b_gu4, layer)
    ys = _gmm_down(block_e, act, w_down, b_down4, slot_w[:, None], layer)
    out = _combine(pos, ys, x.reshape(b * l, d), mod3, 5, l, final_g, final_norm)
    return out.reshape(b, l, d)


def _rope_tables(length):
    rows = length // GRID_W
    row = jnp.repeat(jnp.arange(rows), GRID_W).astype(F32)
    col = jnp.tile(jnp.arange(GRID_W), rows).astype(F32)
    inv = 1.0 / (ROPE_THETA ** (jnp.arange(0, ROPE_AXIS, 2, dtype=F32) / ROPE_AXIS))
    ang_r = row[:, None] * inv[None, :]
    ang_c = col[:, None] * inv[None, :]
    cos = jnp.concatenate([jnp.cos(ang_r)] * 2 + [jnp.cos(ang_c)] * 2, axis=-1)
    sin = jnp.concatenate([jnp.sin(ang_r)] * 2 + [jnp.sin(ang_c)] * 2, axis=-1)
    return cos, sin


def _rotate_half_cols(w):
    a, b_, c_, d_ = jnp.split(w, 4, axis=-1)
    return jnp.concatenate([-b_, a, -d_, c_], axis=-1)


def kernel(x, c, ctx, c_ctx, ada_w, ada_b, norm_mix_g, norm_ffn_g, mla_w_in, mla_q_norm_g,
           mla_kv_norm_g, mla_w_q_up, mla_w_kv_up, mla_w_out, conv_w_in, conv_w, conv_w_out,
           router_w, router_b, expert_w_gu, expert_b_gu, expert_w_down, expert_b_down,
           final_norm_g):
    b, l, d = x.shape
    depth = ada_w.shape[0]
    lc = ctx.shape[1]
    assert depth == 2 and b + 1 <= ADA_ROWS

    cv = jnp.concatenate([c, c_ctx[None, :], jnp.zeros((ADA_ROWS - b - 1, d), F32)], axis=0)
    ctx_row = b
    ada_b3 = ada_b.reshape(depth, 1, N_ADA * d)
    cos, sin = _rope_tables(l)
    b_gu4 = expert_b_gu.reshape(depth, N_EXPERTS, 1, -1)
    b_down4 = expert_b_down.reshape(depth, N_EXPERTS, 1, d)
    w_r = jnp.pad(router_w, ((0, 0), (0, 0), (0, LANES - N_EXPERTS)))
    b_r = jnp.pad(router_b, ((0, 0), (0, LANES - N_EXPERTS)))[:, None, :]
    final_g = final_norm_g[None, :]

    mod3 = _ada(cv, ada_w, ada_b3, 0).reshape(ADA_ROWS, 1, N_ADA * d)
    w_in = mla_w_in[0]
    w_kr = w_in[:, Q_LORA + KV_LORA:]
    w_all = jnp.concatenate([w_in, _rotate_half_cols(w_kr)], axis=1).astype(BF16)
    w_ctx = w_in[:, Q_LORA:].astype(BF16)
    gq = mla_q_norm_g[0][None, :]
    gkv = mla_kv_norm_g[0][None, :]
    g_mix = norm_mix_g[0][None, :]
    wq = mla_w_q_up[0].reshape(Q_LORA, N_HEADS, QK_DIM)
    wq_rope = wq[..., QK_NOPE:]
    wq_all = jnp.concatenate([wq, _rotate_half_cols(wq_rope)], axis=-1)
    wq_all = wq_all.reshape(Q_LORA, N_HEADS * Q_HEAD_COLS).astype(BF16)
    wkv = mla_w_kv_up[0].astype(BF16)
    w_o = mla_w_out[0].astype(BF16)

    cq, ckv, kr = _mla_in(x, g_mix, mod3, 0, w_all, gq, gkv, cos, sin)
    ckv_c, kr_c = _mla_in_ctx(ctx, g_mix, mod3, ctx_row, w_ctx, gkv)
    ckv_all = jnp.concatenate([ckv_c, ckv], axis=1)
    kr_all = jnp.concatenate([kr_c, kr], axis=1)
    q = _q_up(cq, wq_all, cos, sin)
    k, v = _kv_up(ckv_all, kr_all, wkv, 768)
    o = _attention(q, k, v, 256, 768)
    x = _proj_res(o, w_o, x, mod3, 2)
    x = _expert_ffn_residual(x, mod3, 0, norm_ffn_g[0][None, :], w_r[0], b_r[0], expert_w_gu,
                             b_gu4, expert_w_down, b_down4, final_g, False)

    mod3 = _ada(cv, ada_w, ada_b3, 1).reshape(ADA_ROWS, 1, N_ADA * d)
    gb, z = _conv_in(x, norm_mix_g[1][None, :], mod3, 0, conv_w_in[0].astype(BF16))
    x = _conv_out(z, gb, conv_w[0], conv_w_out[0].astype(BF16), x, mod3, 2)
    x = _expert_ffn_residual(x, mod3, 1, norm_ffn_g[1][None, :], w_r[1], b_r[1], expert_w_gu,
                             b_gu4, expert_w_down, b_down4, final_g, True)
    return x
```

```python
import functools

import jax
import jax.numpy as jnp
import numpy as np
from jax import lax
from jax.experimental import pallas as pl
from jax.experimental.pallas import tpu as pltpu

GRID_W = 64
RMS_EPS = 1e-6
N_ADA = 6
N_HEADS = 16
Q_LORA = 512
KV_LORA = 512
QK_NOPE = 128
QK_ROPE = 64
V_DIM = 128
ROPE_AXIS = QK_ROPE // 2
ROPE_THETA = 10000.0
QK_DIM = QK_NOPE + QK_ROPE
ATTN_SCALE = QK_DIM ** -0.5
N_EXPERTS = 32
TOP_K = 4
SWIGLU_LIMIT = 7.0
SWIGLU_ALPHA = 1.702

LANES = 128
SUBLANES = 8
VMEM_LIMIT = 56 * 1024 * 1024

ADA_ROWS = 8
EXPERT_TM = 512
GATHER_ROWS = EXPERT_TM
COMBINE_TOK = 128
ROW_WORDS = SUBLANES
DMA_UNROLL = 8

BF16 = jnp.bfloat16
F32 = jnp.float32


def _cparams(*sem):
    return pltpu.CompilerParams(dimension_semantics=sem, vmem_limit_bytes=VMEM_LIMIT)


def _rms(x):
    return x * lax.rsqrt(jnp.mean(x * x, axis=-1, keepdims=True) + RMS_EPS)


def _modulate(x, g, shift, scale):
    return _rms(x) * g * (1.0 + scale) + shift


def _ada_kernel(cv_ref, w_ref, b_ref, o_ref):
    cv = cv_ref[...]
    s = cv * (1.0 / (1.0 + jnp.exp(-cv)))
    o_ref[...] = jnp.dot(s, w_ref[...], preferred_element_type=F32,
                         precision=lax.Precision.HIGHEST) + b_ref[...]


def _ada(cv, ada_w, ada_b3, layer):
    d = cv.shape[1]
    n = ada_w.shape[2]
    tn = 1024
    return pl.pallas_call(
        _ada_kernel,
        out_shape=jax.ShapeDtypeStruct((ADA_ROWS, n), F32),
        grid=(n // tn,),
        in_specs=[
            pl.BlockSpec((ADA_ROWS, d), lambda j: (0, 0)),
            pl.BlockSpec((None, d, tn), lambda j: (layer, 0, j)),
            pl.BlockSpec((None, 1, tn), lambda j: (layer, 0, j)),
        ],
        out_specs=pl.BlockSpec((ADA_ROWS, tn), lambda j: (0, j)),
        compiler_params=_cparams("arbitrary"),
        name="ada",
    )(cv, ada_w, ada_b3)


def _mla_in_kernel(x_ref, g_ref, sh_ref, sc_ref, w_ref, gq_ref, gkv_ref, cos_ref, sin_ref,
                   cq_ref, ckv_ref, kr_ref):
    h = _modulate(x_ref[0], g_ref[...], sh_ref[0], sc_ref[0]).astype(BF16)
    p = jnp.dot(h, w_ref[...], preferred_element_type=F32)
    cq_ref[0] = (_rms(p[:, :Q_LORA]) * gq_ref[...]).astype(BF16)
    ckv_ref[0] = (_rms(p[:, Q_LORA:Q_LORA + KV_LORA]) * gkv_ref[...]).astype(BF16)
    o = Q_LORA + KV_LORA
    kr = p[:, o:o + QK_ROPE]
    kr_rot = p[:, o + QK_ROPE:o + 2 * QK_ROPE]
    kr_ref[0] = (kr * cos_ref[...] + kr_rot * sin_ref[...]).astype(BF16)


def _mla_in_ctx_kernel(x_ref, g_ref, sh_ref, sc_ref, w_ref, gkv_ref, ckv_ref, kr_ref):
    h = _modulate(x_ref[0], g_ref[...], sh_ref[0], sc_ref[0]).astype(BF16)
    p = jnp.dot(h, w_ref[...], preferred_element_type=F32)
    ckv_ref[0] = (_rms(p[:, :KV_LORA]) * gkv_ref[...]).astype(BF16)
    kr_ref[0] = p[:, KV_LORA:KV_LORA + QK_ROPE].astype(BF16)


def _mla_in(x, g, mod3, layer_chunk0, w_all, gq, gkv, cos, sin):
    b, l, d = x.shape
    tm = 512
    nw = w_all.shape[1]
    row = lambda bi, i: (bi, i, 0)
    return pl.pallas_call(
        _mla_in_kernel,
        out_shape=(jax.ShapeDtypeStruct((b, l, Q_LORA), BF16),
                   jax.ShapeDtypeStruct((b, l, KV_LORA), BF16),
                   jax.ShapeDtypeStruct((b, l, QK_ROPE), BF16)),
        grid=(b, l // tm),
        in_specs=[
            pl.BlockSpec((1, tm, d), row),
            pl.BlockSpec((1, d), lambda bi, i: (0, 0)),
            pl.BlockSpec((1, 1, d), lambda bi, i: (bi, 0, layer_chunk0)),
            pl.BlockSpec((1, 1, d), lambda bi, i: (bi, 0, layer_chunk0 + 1)),
            pl.BlockSpec((d, nw), lambda bi, i: (0, 0)),
            pl.BlockSpec((1, Q_LORA), lambda bi, i: (0, 0)),
            pl.BlockSpec((1, KV_LORA), lambda bi, i: (0, 0)),
            pl.BlockSpec((tm, QK_ROPE), lambda bi, i: (i, 0)),
            pl.BlockSpec((tm, QK_ROPE), lambda bi, i: (i, 0)),
        ],
        out_specs=(pl.BlockSpec((1, tm, Q_LORA), row),
                   pl.BlockSpec((1, tm, KV_LORA), row),
                   pl.BlockSpec((1, tm, QK_ROPE), row)),
        compiler_params=_cparams("arbitrary", "arbitrary"),
        name="mla_in",
    )(x, g, mod3, mod3, w_all, gq, gkv, cos, sin)


def _mla_in_ctx(ctx, g, mod3, ctx_row, w_kv, gkv):
    b, l, d = ctx.shape
    tm = l
    nw = w_kv.shape[1]
    row = lambda bi, i: (bi, i, 0)
    return pl.pallas_call(
        _mla_in_ctx_kernel,
        out_shape=(jax.ShapeDtypeStruct((b, l, KV_LORA), BF16),
                   jax.ShapeDtypeStruct((b, l, QK_ROPE), BF16)),
        grid=(b, l // tm),
        in_specs=[
            pl.BlockSpec((1, tm, d), row),
            pl.BlockSpec((1, d), lambda bi, i: (0, 0)),
            pl.BlockSpec((1, 1, d), lambda bi, i: (ctx_row, 0, 0)),
            pl.BlockSpec((1, 1, d), lambda bi, i: (ctx_row, 0, 1)),
            pl.BlockSpec((d, nw), lambda bi, i: (0, 0)),
            pl.BlockSpec((1, KV_LORA), lambda bi, i: (0, 0)),
        ],
        out_specs=(pl.BlockSpec((1, tm, KV_LORA), row),
                   pl.BlockSpec((1, tm, QK_ROPE), row)),
        compiler_params=_cparams("arbitrary", "arbitrary"),
        name="mla_in_ctx",
    )(ctx, g, mod3, mod3, w_kv, gkv)


Q_HEAD_COLS = QK_NOPE + 2 * QK_ROPE
KV_HEAD_COLS = QK_NOPE + V_DIM


def _q_up_kernel(cq_ref, w_ref, cos_ref, sin_ref, q_ref):
    cq = cq_ref[0]
    cos = cos_ref[...]
    sin = sin_ref[...]
    for h in range(N_HEADS):
        p = jnp.dot(cq, w_ref[:, h * Q_HEAD_COLS:(h + 1) * Q_HEAD_COLS],
                    preferred_element_type=F32)
        rope = p[:, QK_NOPE:QK_NOPE + QK_ROPE] * cos + p[:, QK_NOPE + QK_ROPE:] * sin
        q_ref[0, h, :, :QK_NOPE] = (p[:, :QK_NOPE] * ATTN_SCALE).astype(BF16)
        q_ref[0, h, :, QK_NOPE:] = (rope * ATTN_SCALE).astype(BF16)


def _q_up(cq, w_q, cos, sin):
    b, l, _ = cq.shape
    tm = 512
    return pl.pallas_call(
        _q_up_kernel,
        out_shape=jax.ShapeDtypeStruct((b, N_HEADS, l, QK_DIM), BF16),
        grid=(b, l // tm),
        in_specs=[
            pl.BlockSpec((1, tm, Q_LORA), lambda bi, i: (bi, i, 0)),
            pl.BlockSpec(w_q.shape, lambda bi, i: (0, 0)),
            pl.BlockSpec((tm, QK_ROPE), lambda bi, i: (i, 0)),
            pl.BlockSpec((tm, QK_ROPE), lambda bi, i: (i, 0)),
        ],
        out_specs=pl.BlockSpec((1, N_HEADS, tm, QK_DIM), lambda bi, i: (bi, 0, i, 0)),
        compiler_params=_cparams("arbitrary", "arbitrary"),
        name="q_up",
    )(cq, w_q, cos, sin)


def _kv_up_kernel(ckv_ref, kr_ref, w_ref, k_ref, v_ref):
    ckv = ckv_ref[0]
    kr = kr_ref[0]
    for h in range(N_HEADS):
        p = jnp.dot(ckv, w_ref[:, h * KV_HEAD_COLS:(h + 1) * KV_HEAD_COLS],
                    preferred_element_type=F32)
        k_ref[0, h, :, :QK_NOPE] = p[:, :QK_NOPE].astype(BF16)
        k_ref[0, h, :, QK_NOPE:] = kr
        v_ref[0, h] = p[:, QK_NOPE:].astype(BF16)


def _kv_up(ckv, kr, w_kv, tm):
    b, l, _ = ckv.shape
    return pl.pallas_call(
        _kv_up_kernel,
        out_shape=(jax.ShapeDtypeStruct((b, N_HEADS, l, QK_DIM), BF16),
                   jax.ShapeDtypeStruct((b, N_HEADS, l, V_DIM), BF16)),
        grid=(b, l // tm),
        in_specs=[
            pl.BlockSpec((1, tm, KV_LORA), lambda bi, i: (bi, i, 0)),
            pl.BlockSpec((1, tm, QK_ROPE), lambda bi, i: (bi, i, 0)),
            pl.BlockSpec(w_kv.shape, lambda bi, i: (0, 0)),
        ],
        out_specs=(pl.BlockSpec((1, N_HEADS, tm, QK_DIM), lambda bi, i: (bi, 0, i, 0)),
                   pl.BlockSpec((1, N_HEADS, tm, V_DIM), lambda bi, i: (bi, 0, i, 0))),
        compiler_params=_cparams("arbitrary", "arbitrary"),
        name="kv_up",
    )(ckv, kr, w_kv)


def _attn_kernel(q_ref, k_ref, v_ref, o_ref, *, tk, n_chunks):
    q = q_ref[0, 0]
    tq = q.shape[0]
    n_slabs = tk // LANES
    m = jnp.full((tq, LANES), -jnp.inf, F32)
    l_loc = jnp.zeros((tq, LANES), F32)
    acc = jnp.zeros((tq, V_DIM), F32)
    for c in range(n_chunks):
        k = k_ref[0, 0, c * tk:(c + 1) * tk, :]
        s = lax.dot_general(q, k, (((1,), (1,)), ((), ())), preferred_element_type=F32)
        slabs = [s[:, j * LANES:(j + 1) * LANES] for j in range(n_slabs)]
        m_loc = slabs[0]
        for slab in slabs[1:]:
            m_loc = jnp.maximum(m_loc, slab)
        m_new = jnp.maximum(m, jnp.max(m_loc, axis=-1, keepdims=True))
        alpha = jnp.exp(m - m_new)
        ps = [jnp.exp(slab - m_new) for slab in slabs]
        p_sum = ps[0]
        for p in ps[1:]:
            p_sum = p_sum + p
        l_loc = alpha * l_loc + p_sum
        p_bf = jnp.concatenate([p.astype(BF16) for p in ps], axis=-1)
        acc = alpha * acc + jnp.dot(p_bf, v_ref[0, 0, c * tk:(c + 1) * tk, :],
                                    preferred_element_type=F32)
        m = m_new
    o_ref[0] = (acc / jnp.sum(l_loc, axis=-1, keepdims=True)).astype(BF16)


def _attention(q, k, v, tq, tk):
    b, h, l, _ = q.shape
    lk = k.shape[2]
    return pl.pallas_call(
        functools.partial(_attn_kernel, tk=tk, n_chunks=lk // tk),
        out_shape=jax.ShapeDtypeStruct((b, l, h * V_DIM), BF16),
        grid=(b, h, l // tq),
        in_specs=[
            pl.BlockSpec((1, 1, tq, QK_DIM), lambda bi, hi, i: (bi, hi, i, 0)),
            pl.BlockSpec((1, 1, lk, QK_DIM), lambda bi, hi, i: (bi, hi, 0, 0)),
            pl.BlockSpec((1, 1, lk, V_DIM), lambda bi, hi, i: (bi, hi, 0, 0)),
        ],
        out_specs=pl.BlockSpec((1, tq, V_DIM), lambda bi, hi, i: (bi, i, hi)),
        compiler_params=_cparams("arbitrary", "arbitrary", "arbitrary"),
        name="attention",
    )(q, k, v)


def _proj_res_kernel(o_ref, w_ref, x_ref, g1_ref, out_ref):
    mix = jnp.dot(o_ref[0], w_ref[...], preferred_element_type=F32)
    out_ref[0] = x_ref[0] + g1_ref[0] * mix


def _proj_res(o, w_out, x, mod3, gate_chunk):
    b, l, d = x.shape
    tm = 512
    row = lambda bi, i: (bi, i, 0)
    return pl.pallas_call(
        _proj_res_kernel,
        out_shape=jax.ShapeDtypeStruct((b, l, d), F32),
        grid=(b, l // tm),
        in_specs=[
            pl.BlockSpec((1, tm, o.shape[2]), row),
            pl.BlockSpec(w_out.shape, lambda bi, i: (0, 0)),
            pl.BlockSpec((1, tm, d), row),
            pl.BlockSpec((1, 1, d), lambda bi, i: (bi, 0, gate_chunk)),
        ],
        out_specs=pl.BlockSpec((1, tm, d), row),
        compiler_params=_cparams("arbitrary", "arbitrary"),
        name="proj_res",
    )(o, w_out, x, mod3)


def _conv_in_kernel(x_ref, g_ref, sh_ref, sc_ref, wb_ref, wc_ref, wu_ref, gb_ref, z_ref, h_ref):
    @pl.when(pl.program_id(2) == 0)
    def _():
        h_ref[...] = _modulate(x_ref[0], g_ref[...], sh_ref[0], sc_ref[0]).astype(BF16)

    h = h_ref[...]
    gb_ref[0] = jnp.dot(h, wb_ref[...], preferred_element_type=F32).astype(BF16)
    gc = jnp.dot(h, wc_ref[...], preferred_element_type=F32)
    u = jnp.dot(h, wu_ref[...], preferred_element_type=F32)
    z_ref[0] = (gc * u).astype(BF16)


def _conv_in(x, g, mod3, chunk0, w_in):
    b, l, d = x.shape
    tm, tn = 512, 512
    nn = d // tn
    row = lambda bi, i, j: (bi, i, 0)
    col = lambda bi, i, j: (bi, i, j)
    return pl.pallas_call(
        _conv_in_kernel,
        out_shape=(jax.ShapeDtypeStruct((b, l, d), BF16), jax.ShapeDtypeStruct((b, l, d), BF16)),
        grid=(b, l // tm, nn),
        in_specs=[
            pl.BlockSpec((1, tm, d), row),
            pl.BlockSpec((1, d), lambda bi, i, j: (0, 0)),
            pl.BlockSpec((1, 1, d), lambda bi, i, j: (bi, 0, chunk0)),
            pl.BlockSpec((1, 1, d), lambda bi, i, j: (bi, 0, chunk0 + 1)),
            pl.BlockSpec((d, tn), lambda bi, i, j: (0, j)),
            pl.BlockSpec((d, tn), lambda bi, i, j: (0, nn + j)),
            pl.BlockSpec((d, tn), lambda bi, i, j: (0, 2 * nn + j)),
        ],
        out_specs=(pl.BlockSpec((1, tm, tn), col), pl.BlockSpec((1, tm, tn), col)),
        scratch_shapes=[pltpu.VMEM((tm, d), BF16)],
        compiler_params=_cparams("arbitrary", "arbitrary", "arbitrary"),
        name="conv_in",
    )(x, g, mod3, mod3, w_in, w_in, w_in)


def _conv_out_kernel(z_ref, zp_ref, zn_ref, gb_ref, cw_ref, w_ref, x_ref, g1_ref, out_ref):
    i = pl.program_id(1)
    last = pl.num_programs(1) - 1
    z = z_ref[0].astype(F32)
    tm = z.shape[0]
    prev_row = jnp.where(i > 0, zp_ref[0, SUBLANES - 1:SUBLANES, :].astype(F32), 0.0)
    next_row = jnp.where(i < last, zn_ref[0, 0:1, :].astype(F32), 0.0)
    ridx = lax.broadcasted_iota(jnp.int32, z.shape, 0)
    z_prev = jnp.where(ridx == 0, prev_row, pltpu.roll(z, 1, 0))
    z_next = jnp.where(ridx == tm - 1, next_row, pltpu.roll(z, tm - 1, 0))
    conv = cw_ref[0:1, :] * z_prev + cw_ref[1:2, :] * z + cw_ref[2:3, :] * z_next
    y = (gb_ref[0].astype(F32) * conv).astype(BF16)
    mix = jnp.dot(y, w_ref[...], preferred_element_type=F32)
    out_ref[0] = x_ref[0] + g1_ref[0] * mix


def _conv_out(z, gb, conv_w, w_out, x, mod3, gate_chunk):
    b, l, d = x.shape
    tm = 256
    hb = tm // SUBLANES
    n_halo = l // SUBLANES
    row = lambda bi, i: (bi, i, 0)
    return pl.pallas_call(
        _conv_out_kernel,
        out_shape=jax.ShapeDtypeStruct((b, l, d), F32),
        grid=(b, l // tm),
        in_specs=[
            pl.BlockSpec((1, tm, d), row),
            pl.BlockSpec((1, SUBLANES, d), lambda bi, i: (bi, jnp.maximum(i * hb - 1, 0), 0)),
            pl.BlockSpec((1, SUBLANES, d),
                         lambda bi, i: (bi, jnp.minimum((i + 1) * hb, n_halo - 1), 0)),
            pl.BlockSpec((1, tm, d), row),
            pl.BlockSpec(conv_w.shape, lambda bi, i: (0, 0)),
            pl.BlockSpec(w_out.shape, lambda bi, i: (0, 0)),
            pl.BlockSpec((1, tm, d), row),
            pl.BlockSpec((1, 1, d), lambda bi, i: (bi, 0, gate_chunk)),
        ],
        out_specs=pl.BlockSpec((1, tm, d), row),
        compiler_params=_cparams("arbitrary", "arbitrary"),
        name="conv_out",
    )(z, z, z, gb, conv_w, w_out, x, mod3)


def _pack_rows(h, hp_ref):
    tm, d = h.shape
    bits = pltpu.bitcast(h.astype(BF16).astype(F32), jnp.uint32)
    words = (bits[:, :d // 2] >> 16) | (bits[:, d // 2:] & jnp.uint32(0xFFFF0000))
    for s in range(ROW_WORDS):
        hp_ref[pl.ds(s, tm, stride=ROW_WORDS), :] = words[:, s * LANES:(s + 1) * LANES]


def _unpack_rows(buf_ref, n_rows):
    lows, highs = [], []
    for s in range(ROW_WORDS):
        w = buf_ref[pl.ds(s, n_rows, stride=ROW_WORDS), :]
        lows.append(pltpu.bitcast(w << 16, F32).astype(BF16))
        highs.append(pltpu.bitcast(w & jnp.uint32(0xFFFF0000), F32).astype(BF16))
    return jnp.concatenate(lows + highs, axis=-1)


def _router_kernel(x_ref, g_ref, sh_ref, sc_ref, wr_ref, br_ref, hp_ref, idx_ref, gate_ref):
    h = _modulate(x_ref[0], g_ref[...], sh_ref[0], sc_ref[0])
    _pack_rows(h, hp_ref)
    logits = jnp.dot(h, wr_ref[...], preferred_element_type=F32,
                     precision=lax.Precision.HIGHEST) + br_ref[...]
    lane = lax.broadcasted_iota(jnp.int32, logits.shape, 1).astype(F32)
    work = jnp.where(lane < N_EXPERTS, logits, -jnp.inf)
    vals, idxs = [], []
    for _ in range(TOP_K):
        mx = jnp.max(work, axis=-1, keepdims=True)
        ix = jnp.min(jnp.where(work == mx, lane, float(LANES)), axis=-1, keepdims=True)
        vals.append(mx)
        idxs.append(ix.astype(jnp.int32))
        work = jnp.where(lane == ix, -jnp.inf, work)
    ex = [jnp.exp(v - vals[0]) for v in vals]
    den = ex[0] + ex[1] + ex[2] + ex[3]
    for k in range(TOP_K):
        idx_ref[:, k:k + 1] = idxs[k]
        gate_ref[:, k:k + 1] = ex[k] / den


def _router(x, g, mod3, chunk0, w_r, b_r):
    b, l, d = x.shape
    tm = 512
    nt = l // tm
    t = b * l
    return pl.pallas_call(
        _router_kernel,
        out_shape=(jax.ShapeDtypeStruct((t * ROW_WORDS, LANES), jnp.uint32),
                   jax.ShapeDtypeStruct((t, TOP_K), jnp.int32),
                   jax.ShapeDtypeStruct((t, TOP_K), F32)),
        grid=(b, nt),
        in_specs=[
            pl.BlockSpec((1, tm, d), lambda bi, i: (bi, i, 0)),
            pl.BlockSpec((1, d), lambda bi, i: (0, 0)),
            pl.BlockSpec((1, 1, d), lambda bi, i: (bi, 0, chunk0)),
            pl.BlockSpec((1, 1, d), lambda bi, i: (bi, 0, chunk0 + 1)),
            pl.BlockSpec((d, LANES), lambda bi, i: (0, 0)),
            pl.BlockSpec((1, LANES), lambda bi, i: (0, 0)),
        ],
        out_specs=(pl.BlockSpec((tm * ROW_WORDS, LANES), lambda bi, i: (bi * nt + i, 0)),
                   pl.BlockSpec((tm, TOP_K), lambda bi, i: (bi * nt + i, 0)),
                   pl.BlockSpec((tm, TOP_K), lambda bi, i: (bi * nt + i, 0))),
        compiler_params=_cparams("arbitrary", "arbitrary"),
        name="router",
    )(x, g, mod3, mod3, w_r, b_r)


def _issue_row_copies(n_rows, make_copy):
    def issue(it, carry):
        for u in range(DMA_UNROLL):
            make_copy(it * DMA_UNROLL + u).start()
        return carry

    lax.fori_loop(0, n_rows // DMA_UNROLL, issue, 0)


def _gather_kernel(tok_ref, tokn_ref, hp_ref, xs_ref, buf_ref, sem):
    i = pl.program_id(0)
    n = pl.num_programs(0)
    slot = lax.rem(i, 2)

    def issue_from(t_ref, dst_slot):
        def make_copy(r):
            src = pl.multiple_of(t_ref[0, 0, r] * ROW_WORDS, ROW_WORDS)
            return pltpu.make_async_copy(hp_ref.at[pl.ds(src, ROW_WORDS)],
                                         buf_ref.at[dst_slot, pl.ds(r * ROW_WORDS, ROW_WORDS)],
                                         sem.at[dst_slot])
        _issue_row_copies(GATHER_ROWS, make_copy)

    @pl.when(i == 0)
    def _():
        issue_from(tok_ref, 0)

    @pl.when(i + 1 < n)
    def _():
        issue_from(tokn_ref, 1 - slot)

    pltpu.make_async_copy(hp_ref.at[pl.ds(0, GATHER_ROWS * ROW_WORDS)], buf_ref.at[slot],
                          sem.at[slot]).wait()
    xs_ref[...] = _unpack_rows(buf_ref.at[slot], GATHER_ROWS)


def _gather_rows(slot_tok, hp, d):
    n_slots = slot_tok.shape[0]
    n_steps = n_slots // GATHER_ROWS
    tok3 = slot_tok.reshape(n_steps, 1, GATHER_ROWS)
    return pl.pallas_call(
        _gather_kernel,
        out_shape=jax.ShapeDtypeStruct((n_slots, d), BF16),
        grid=(n_steps,),
        in_specs=[
            pl.BlockSpec((1, 1, GATHER_ROWS), lambda i: (i, 0, 0), memory_space=pltpu.SMEM),
            pl.BlockSpec((1, 1, GATHER_ROWS), lambda i: (jnp.minimum(i + 1, n_steps - 1), 0, 0),
                         memory_space=pltpu.SMEM),
            pl.BlockSpec(memory_space=pl.ANY),
        ],
        out_specs=pl.BlockSpec((GATHER_ROWS, d), lambda i: (i, 0)),
        scratch_shapes=[pltpu.VMEM((2, GATHER_ROWS * ROW_WORDS, LANES), jnp.uint32),
                        pltpu.SemaphoreType.DMA((2,))],
        compiler_params=_cparams("arbitrary"),
        name="gather_rows",
    )(tok3, tok3, hp)


def _expert_changed(be_ref):
    bi = pl.program_id(1)
    prev = be_ref[jnp.maximum(bi - 1, 0)]
    return jnp.logical_or(bi == 0, be_ref[bi] != prev)


def _gmm_gu_kernel(be_ref, x_ref, wg_ref, wl_ref, bg_ref, bl_ref, act_ref, wg_s, wl_s):
    @pl.when(_expert_changed(be_ref))
    def _():
        wg_s[...] = wg_ref[...].astype(BF16)
        wl_s[...] = wl_ref[...].astype(BF16)

    xb = x_ref[...]
    glu = jnp.dot(xb, wg_s[...], preferred_element_type=F32) + bg_ref[...]
    lin = jnp.dot(xb, wl_s[...], preferred_element_type=F32) + bl_ref[...]
    glu = jnp.minimum(glu, SWIGLU_LIMIT)
    lin = jnp.clip(lin, -SWIGLU_LIMIT, SWIGLU_LIMIT)
    sig = 1.0 / (1.0 + jnp.exp(-SWIGLU_ALPHA * glu))
    act_ref[...] = (glu * sig * (lin + 1.0)).astype(BF16)


def _gmm_gu(block_e, xs, w_gu, b_gu4, layer):
    n_slots, d = xs.shape
    f = w_gu.shape[3] // 2
    tm, tn = EXPERT_TM, 1024
    nn = f // tn
    return pl.pallas_call(
        _gmm_gu_kernel,
        out_shape=jax.ShapeDtypeStruct((n_slots, f), BF16),
        grid_spec=pltpu.PrefetchScalarGridSpec(
            num_scalar_prefetch=1,
            grid=(nn, n_slots // tm),
            in_specs=[
                pl.BlockSpec((tm, d), lambda j, i, be: (i, 0)),
                pl.BlockSpec((None, None, d, tn), lambda j, i, be: (layer, be[i], 0, j)),
                pl.BlockSpec((None, None, d, tn), lambda j, i, be: (layer, be[i], 0, nn + j)),
                pl.BlockSpec((None, None, 1, tn), lambda j, i, be: (layer, be[i], 0, j)),
                pl.BlockSpec((None, None, 1, tn), lambda j, i, be: (layer, be[i], 0, nn + j)),
            ],
            out_specs=pl.BlockSpec((tm, tn), lambda j, i, be: (i, j)),
            scratch_shapes=[pltpu.VMEM((d, tn), BF16), pltpu.VMEM((d, tn), BF16)],
        ),
        compiler_params=_cparams("arbitrary", "arbitrary"),
        name="gmm_gate_up",
    )(block_e, xs, w_gu, w_gu, b_gu4, b_gu4)


DOWN_TN = SUBLANES * LANES


def _gmm_down_kernel(be_ref, a_ref, w_ref, b_ref, y_ref, w_s):
    @pl.when(_expert_changed(be_ref))
    def _():
        w_s[...] = w_ref[...].astype(BF16)

    y = jnp.dot(a_ref[...], w_s[...], preferred_element_type=F32) + b_ref[...]
    tm = y.shape[0]
    y2_ref = y_ref.reshape(tm * SUBLANES, LANES)
    for cc in range(SUBLANES):
        y2_ref[pl.ds(cc, tm, stride=SUBLANES), :] = y[:, cc * LANES:(cc + 1) * LANES]


def _gmm_down(block_e, act, w_down, b_down4, layer):
    n_slots, f = act.shape
    d = w_down.shape[3]
    tm, tn = EXPERT_TM, DOWN_TN
    return pl.pallas_call(
        _gmm_down_kernel,
        out_shape=jax.ShapeDtypeStruct((n_slots, d // LANES, LANES), F32),
        grid_spec=pltpu.PrefetchScalarGridSpec(
            num_scalar_prefetch=1,
            grid=(d // tn, n_slots // tm),
            in_specs=[
                pl.BlockSpec((tm, f), lambda j, i, be: (i, 0)),
                pl.BlockSpec((None, None, f, tn), lambda j, i, be: (layer, be[i], 0, j)),
                pl.BlockSpec((None, None, 1, tn), lambda j, i, be: (layer, be[i], 0, j)),
            ],
            out_specs=pl.BlockSpec((tm, SUBLANES, LANES), lambda j, i, be: (i, j, 0)),
            scratch_shapes=[pltpu.VMEM((f, tn), BF16)],
        ),
        compiler_params=_cparams("arbitrary", "arbitrary"),
        name="gmm_down",
    )(block_e, act, w_down, b_down4)


COMBINE_ROWS = COMBINE_TOK * TOP_K


def _combine_kernel(pos_ref, posn_ref, ys_ref, gate_ref, x_ref, g2_ref, gf_ref, out_ref,
                    buf_ref, sem, *, final_norm):
    i = pl.program_id(0)
    n = pl.num_programs(0)
    slot = lax.rem(i, 2)
    n_chunks = x_ref.shape[1] // LANES

    def issue_from(p_ref, dst_slot):
        def make_copy(r):
            src = pl.multiple_of(p_ref[0, 0, r] * n_chunks, n_chunks)
            return pltpu.make_async_copy(ys_ref.at[pl.ds(src, n_chunks)],
                                         buf_ref.at[dst_slot, pl.ds(r * n_chunks, n_chunks)],
                                         sem.at[dst_slot])
        _issue_row_copies(COMBINE_ROWS, make_copy)

    @pl.when(i == 0)
    def _():
        issue_from(pos_ref, 0)

    @pl.when(i + 1 < n)
    def _():
        issue_from(posn_ref, 1 - slot)

    pltpu.make_async_copy(ys_ref.at[pl.ds(0, COMBINE_ROWS * n_chunks)], buf_ref.at[slot],
                          sem.at[slot]).wait()

    rows_ref = buf_ref.at[slot]
    gates = [jnp.broadcast_to(gate_ref[:, k:k + 1], (COMBINE_TOK, LANES)) for k in range(TOP_K)]
    ssq = jnp.zeros((COMBINE_TOK, LANES), F32)
    for c in range(n_chunks):
        cols = slice(c * LANES, (c + 1) * LANES)
        y = None
        for k in range(TOP_K):
            rows = rows_ref[pl.ds(k * COMBINE_TOK * n_chunks + c, COMBINE_TOK, stride=n_chunks), :]
            y = gates[k] * rows if y is None else y + gates[k] * rows
        o = x_ref[:, cols] + g2_ref[0, :, cols] * y
        out_ref[:, cols] = o
        ssq = ssq + o * o
    if final_norm:
        d = n_chunks * LANES
        inv = lax.rsqrt(jnp.sum(ssq, axis=-1, keepdims=True) / d + RMS_EPS)
        out_ref[...] = out_ref[...] * inv * gf_ref[...]


def _combine(pos, ys2, gate, x2, mod3, gate_chunk, tokens_per_batch, final_g, final_norm):
    t, d = x2.shape
    n_chunks = d // LANES
    n_steps = t // COMBINE_TOK
    steps_per_batch = tokens_per_batch // COMBINE_TOK
    pos3 = pos.reshape(n_steps, COMBINE_TOK, TOP_K).transpose(0, 2, 1).reshape(n_steps, 1, COMBINE_ROWS)
    return pl.pallas_call(
        functools.partial(_combine_kernel, final_norm=final_norm),
        out_shape=jax.ShapeDtypeStruct((t, d), F32),
        grid=(n_steps,),
        in_specs=[
            pl.BlockSpec((1, 1, COMBINE_ROWS), lambda i: (i, 0, 0), memory_space=pltpu.SMEM),
            pl.BlockSpec((1, 1, COMBINE_ROWS), lambda i: (jnp.minimum(i + 1, n_steps - 1), 0, 0),
                         memory_space=pltpu.SMEM),
            pl.BlockSpec(memory_space=pl.ANY),
            pl.BlockSpec((COMBINE_TOK, TOP_K), lambda i: (i, 0)),
            pl.BlockSpec((COMBINE_TOK, d), lambda i: (i, 0)),
            pl.BlockSpec((1, 1, d), lambda i: (i // steps_per_batch, 0, gate_chunk)),
            pl.BlockSpec((1, d), lambda i: (0, 0)),
        ],
        out_specs=pl.BlockSpec((COMBINE_TOK, d), lambda i: (i, 0)),
        scratch_shapes=[pltpu.VMEM((2, COMBINE_ROWS * n_chunks, LANES), F32),
                        pltpu.SemaphoreType.DMA((2,))],
        compiler_params=_cparams("arbitrary"),
        name="combine",
    )(pos3, pos3, ys2, gate, x2, mod3, final_g)


def _slot_plan(top_idx):
    t = top_idx.shape[0]
    n_assign = t * TOP_K
    flat_e = top_idx.reshape(-1)
    onehot = (flat_e[:, None] == jnp.arange(N_EXPERTS, dtype=jnp.int32)[None, :]).astype(jnp.int32)
    csum = jnp.cumsum(onehot, axis=0)
    counts = csum[-1]
    rank = jnp.sum(csum * onehot, axis=1) - 1
    padded = (counts + EXPERT_TM - 1) // EXPERT_TM * EXPERT_TM
    padded_end = jnp.cumsum(padded)
    padded_start = padded_end - padded
    pos = (padded_start[flat_e] + rank).astype(jnp.int32)
    n_blocks = -(-(n_assign + N_EXPERTS * (EXPERT_TM - 1)) // EXPERT_TM)
    n_blocks = -(-n_blocks * EXPERT_TM // GATHER_ROWS) * GATHER_ROWS // EXPERT_TM
    n_slots = n_blocks * EXPERT_TM
    flat_tok = jnp.arange(n_assign, dtype=jnp.int32) // TOP_K
    slot_tok = jnp.zeros((n_slots,), jnp.int32).at[pos].set(flat_tok)
    block_start = jnp.arange(n_blocks, dtype=jnp.int32) * EXPERT_TM
    block_e = jnp.minimum(jnp.sum(block_start[:, None] >= padded_end[None, :], axis=1),
                          N_EXPERTS - 1).astype(jnp.int32)
    return pos, slot_tok, block_e


def _expert_ffn_residual(x, mod3, layer, norm_g, w_r, b_r, w_gu, b_gu4, w_down, b_down4,
                         final_g, final_norm):
    b, l, d = x.shape
    assert d == 2 * ROW_WORDS * LANES
    hp, top_idx, gate = _router(x, norm_g, mod3, 3, w_r, b_r)
    pos, slot_tok, block_e = _slot_plan(top_idx)
    xs = _gather_rows(slot_tok, hp, d)
    act = _gmm_gu(block_e, xs, w_gu, b_gu4, layer)
    ys3 = _gmm_down(block_e, act, w_down, b_down4, layer)
    ys2 = ys3.reshape(ys3.shape[0] * ys3.shape[1], LANES)
    out = _combine(pos, ys2, gate, x.reshape(b * l, d), mod3, 5, l, final_g, final_norm)
    return out.reshape(b, l, d)


def _rope_tables(length):
    rows = length // GRID_W
    row = jnp.repeat(jnp.arange(rows), GRID_W).astype(F32)
    col = jnp.tile(jnp.arange(GRID_W), rows).astype(F32)
    inv = 1.0 / (ROPE_THETA ** (jnp.arange(0, ROPE_AXIS, 2, dtype=F32) / ROPE_AXIS))
    ang_r = row[:, None] * inv[None, :]
    ang_c = col[:, None] * inv[None, :]
    cos = jnp.concatenate([jnp.cos(ang_r)] * 2 + [jnp.cos(ang_c)] * 2, axis=-1)
    sin = jnp.concatenate([jnp.sin(ang_r)] * 2 + [jnp.sin(ang_c)] * 2, axis=-1)
    return cos, sin


def _rotate_half_cols(w):
    a, b_, c_, d_ = jnp.split(w, 4, axis=-1)
    return jnp.concatenate([-b_, a, -d_, c_], axis=-1)


def kernel(x, c, ctx, c_ctx, ada_w, ada_b, norm_mix_g, norm_ffn_g, mla_w_in, mla_q_norm_g,
           mla_kv_norm_g, mla_w_q_up, mla_w_kv_up, mla_w_out, conv_w_in, conv_w, conv_w_out,
           router_w, router_b, expert_w_gu, expert_b_gu, expert_w_down, expert_b_down,
           final_norm_g):
    b, l, d = x.shape
    depth = ada_w.shape[0]
    lc = ctx.shape[1]
    assert depth == 2 and b + 1 <= ADA_ROWS

    cv = jnp.concatenate([c, c_ctx[None, :], jnp.zeros((ADA_ROWS - b - 1, d), F32)], axis=0)
    ctx_row = b
    ada_b3 = ada_b.reshape(depth, 1, N_ADA * d)
    cos, sin = _rope_tables(l)
    b_gu4 = expert_b_gu.reshape(depth, N_EXPERTS, 1, -1)
    b_down4 = expert_b_down.reshape(depth, N_EXPERTS, 1, d)
    w_r = jnp.pad(router_w, ((0, 0), (0, 0), (0, LANES - N_EXPERTS)))
    b_r = jnp.pad(router_b, ((0, 0), (0, LANES - N_EXPERTS)))[:, None, :]
    final_g = final_norm_g[None, :]

    mod3 = _ada(cv, ada_w, ada_b3, 0).reshape(ADA_ROWS, 1, N_ADA * d)
    w_in = mla_w_in[0]
    w_kr = w_in[:, Q_LORA + KV_LORA:]
    w_all = jnp.concatenate([w_in, _rotate_half_cols(w_kr)], axis=1).astype(BF16)
    w_ctx = w_in[:, Q_LORA:].astype(BF16)
    gq = mla_q_norm_g[0][None, :]
    gkv = mla_kv_norm_g[0][None, :]
    g_mix = norm_mix_g[0][None, :]
    wq = mla_w_q_up[0].reshape(Q_LORA, N_HEADS, QK_DIM)
    wq_rope = wq[..., QK_NOPE:]
    wq_all = jnp.concatenate([wq, _rotate_half_cols(wq_rope)], axis=-1)
    wq_all = wq_all.reshape(Q_LORA, N_HEADS * Q_HEAD_COLS).astype(BF16)
    wkv = mla_w_kv_up[0].astype(BF16)
    w_o = mla_w_out[0].astype(BF16)

    cq, ckv, kr = _mla_in(x, g_mix, mod3, 0, w_all, gq, gkv, cos, sin)
    ckv_c, kr_c = _mla_in_ctx(ctx, g_mix, mod3, ctx_row, w_ctx, gkv)
    ckv_all = jnp.concatenate([ckv_c, ckv], axis=1)
    kr_all = jnp.concatenate([kr_c, kr], axis=1)
    q = _q_up(cq, wq_all, cos, sin)
    k, v = _kv_up(ckv_all, kr_all, wkv, 768)
    o = _attention(q, k, v, 512, 768)
    x = _proj_res(o, w_o, x, mod3, 2)
    x = _expert_ffn_residual(x, mod3, 0, norm_ffn_g[0][None, :], w_r[0], b_r[0], expert_w_gu,
                             b_gu4, expert_w_down, b_down4, final_g, False)

    mod3 = _ada(cv, ada_w, ada_b3, 1).reshape(ADA_ROWS, 1, N_ADA * d)
    gb, z = _conv_in(x, norm_mix_g[1][None, :], mod3, 0, conv_w_in[0].astype(BF16))
    x = _conv_out(z, gb, conv_w[0], conv_w_out[0].astype(BF16), x, mod3, 2)
    x = _expert_ffn_residual(x, mod3, 1, norm_ffn_g[1][None, :], w_r[1], b_r[1], expert_w_gu,
                             b_gu4, expert_w_down, b_down4, final_g, True)
    return x
```

```python
import functools

import jax
import jax.numpy as jnp
import numpy as np
from jax import lax
from jax.experimental import pallas as pl
from jax.experimental.pallas import tpu as pltpu

GRID_W = 64
RMS_EPS = 1e-6
N_ADA = 6
N_HEADS = 16
Q_LORA = 512
KV_LORA = 512
QK_NOPE = 128
QK_ROPE = 64
V_DIM = 128
ROPE_AXIS = QK_ROPE // 2
ROPE_THETA = 10000.0
QK_DIM = QK_NOPE + QK_ROPE
ATTN_SCALE = QK_DIM ** -0.5
N_EXPERTS = 32
TOP_K = 4
SWIGLU_LIMIT = 7.0
SWIGLU_ALPHA = 1.702

LANES = 128
SUBLANES = 8
VMEM_LIMIT = 56 * 1024 * 1024

ADA_ROWS = 8
EXPERT_TM = 512
GATHER_ROWS = EXPERT_TM
COMBINE_TOK = 128
ROW_WORDS = SUBLANES
DMA_UNROLL = 8

BF16 = jnp.bfloat16
F32 = jnp.float32


def _cparams(*sem):
    return pltpu.CompilerParams(dimension_semantics=sem, vmem_limit_bytes=VMEM_LIMIT)


def _rms(x):
    return x * lax.rsqrt(jnp.mean(x * x, axis=-1, keepdims=True) + RMS_EPS)


def _modulate(x, g, shift, scale):
    return _rms(x) * g * (1.0 + scale) + shift


def _ada_kernel(cv_ref, w_ref, b_ref, o_ref):
    cv = cv_ref[...]
    s = cv * (1.0 / (1.0 + jnp.exp(-cv)))
    o_ref[...] = jnp.dot(s, w_ref[...], preferred_element_type=F32,
                         precision=lax.Precision.HIGHEST) + b_ref[...]


def _ada(cv, ada_w, ada_b3, layer):
    d = cv.shape[1]
    n = ada_w.shape[2]
    tn = 1024
    return pl.pallas_call(
        _ada_kernel,
        out_shape=jax.ShapeDtypeStruct((ADA_ROWS, n), F32),
        grid=(n // tn,),
        in_specs=[
            pl.BlockSpec((ADA_ROWS, d), lambda j: (0, 0)),
            pl.BlockSpec((None, d, tn), lambda j: (layer, 0, j)),
            pl.BlockSpec((None, 1, tn), lambda j: (layer, 0, j)),
        ],
        out_specs=pl.BlockSpec((ADA_ROWS, tn), lambda j: (0, j)),
        compiler_params=_cparams("arbitrary"),
        name="ada",
    )(cv, ada_w, ada_b3)


def _mla_in_kernel(x_ref, g_ref, sh_ref, sc_ref, w_ref, gq_ref, gkv_ref, cos_ref, sin_ref,
                   cq_ref, ckv_ref, kr_ref):
    h = _modulate(x_ref[0], g_ref[...], sh_ref[0], sc_ref[0]).astype(BF16)
    p = jnp.dot(h, w_ref[...], preferred_element_type=F32)
    cq_ref[0] = (_rms(p[:, :Q_LORA]) * gq_ref[...]).astype(BF16)
    ckv_ref[0] = (_rms(p[:, Q_LORA:Q_LORA + KV_LORA]) * gkv_ref[...]).astype(BF16)
    o = Q_LORA + KV_LORA
    kr = p[:, o:o + QK_ROPE]
    kr_rot = p[:, o + QK_ROPE:o + 2 * QK_ROPE]
    kr_ref[0] = (kr * cos_ref[...] + kr_rot * sin_ref[...]).astype(BF16)


def _mla_in_ctx_kernel(x_ref, g_ref, sh_ref, sc_ref, w_ref, gkv_ref, ckv_ref, kr_ref):
    h = _modulate(x_ref[0], g_ref[...], sh_ref[0], sc_ref[0]).astype(BF16)
    p = jnp.dot(h, w_ref[...], preferred_element_type=F32)
    ckv_ref[0] = (_rms(p[:, :KV_LORA]) * gkv_ref[...]).astype(BF16)
    kr_ref[0] = p[:, KV_LORA:KV_LORA + QK_ROPE].astype(BF16)


def _mla_in(x, g, mod3, layer_chunk0, w_all, gq, gkv, cos, sin):
    b, l, d = x.shape
    tm = 512
    nw = w_all.shape[1]
    row = lambda bi, i: (bi, i, 0)
    return pl.pallas_call(
        _mla_in_kernel,
        out_shape=(jax.ShapeDtypeStruct((b, l, Q_LORA), BF16),
                   jax.ShapeDtypeStruct((b, l, KV_LORA), BF16),
                   jax.ShapeDtypeStruct((b, l, QK_ROPE), BF16)),
        grid=(b, l // tm),
        in_specs=[
            pl.BlockSpec((1, tm, d), row),
            pl.BlockSpec((1, d), lambda bi, i: (0, 0)),
            pl.BlockSpec((1, 1, d), lambda bi, i: (bi, 0, layer_chunk0)),
            pl.BlockSpec((1, 1, d), lambda bi, i: (bi, 0, layer_chunk0 + 1)),
            pl.BlockSpec((d, nw), lambda bi, i: (0, 0)),
            pl.BlockSpec((1, Q_LORA), lambda bi, i: (0, 0)),
            pl.BlockSpec((1, KV_LORA), lambda bi, i: (0, 0)),
            pl.BlockSpec((tm, QK_ROPE), lambda bi, i: (i, 0)),
            pl.BlockSpec((tm, QK_ROPE), lambda bi, i: (i, 0)),
        ],
        out_specs=(pl.BlockSpec((1, tm, Q_LORA), row),
                   pl.BlockSpec((1, tm, KV_LORA), row),
                   pl.BlockSpec((1, tm, QK_ROPE), row)),
        compiler_params=_cparams("arbitrary", "arbitrary"),
        name="mla_in",
    )(x, g, mod3, mod3, w_all, gq, gkv, cos, sin)


def _mla_in_ctx(ctx, g, mod3, ctx_row, w_kv, gkv):
    b, l, d = ctx.shape
    tm = l
    nw = w_kv.shape[1]
    row = lambda bi, i: (bi, i, 0)
    return pl.pallas_call(
        _mla_in_ctx_kernel,
        out_shape=(jax.ShapeDtypeStruct((b, l, KV_LORA), BF16),
                   jax.ShapeDtypeStruct((b, l, QK_ROPE), BF16)),
        grid=(b, l // tm),
        in_specs=[
            pl.BlockSpec((1, tm, d), row),
            pl.BlockSpec((1, d), lambda bi, i: (0, 0)),
            pl.BlockSpec((1, 1, d), lambda bi, i: (ctx_row, 0, 0)),
            pl.BlockSpec((1, 1, d), lambda bi, i: (ctx_row, 0, 1)),
            pl.BlockSpec((d, nw), lambda bi, i: (0, 0)),
            pl.BlockSpec((1, KV_LORA), lambda bi, i: (0, 0)),
        ],
        out_specs=(pl.BlockSpec((1, tm, KV_LORA), row),
                   pl.BlockSpec((1, tm, QK_ROPE), row)),
        compiler_params=_cparams("arbitrary", "arbitrary"),
        name="mla_in_ctx",
    )(ctx, g, mod3, mod3, w_kv, gkv)


Q_HEAD_COLS = QK_NOPE + 2 * QK_ROPE
KV_HEAD_COLS = QK_NOPE + V_DIM


def _q_up_kernel(cq_ref, w_ref, cos_ref, sin_ref, q_ref):
    cq = cq_ref[0]
    cos = cos_ref[...]
    sin = sin_ref[...]
    for h in range(N_HEADS):
        p = jnp.dot(cq, w_ref[:, h * Q_HEAD_COLS:(h + 1) * Q_HEAD_COLS],
                    preferred_element_type=F32)
        rope = p[:, QK_NOPE:QK_NOPE + QK_ROPE] * cos + p[:, QK_NOPE + QK_ROPE:] * sin
        q_ref[0, h, :, :QK_NOPE] = (p[:, :QK_NOPE] * ATTN_SCALE).astype(BF16)
        q_ref[0, h, :, QK_NOPE:] = (rope * ATTN_SCALE).astype(BF16)


def _q_up(cq, w_q, cos, sin):
    b, l, _ = cq.shape
    tm = 512
    return pl.pallas_call(
        _q_up_kernel,
        out_shape=jax.ShapeDtypeStruct((b, N_HEADS, l, QK_DIM), BF16),
        grid=(b, l // tm),
        in_specs=[
            pl.BlockSpec((1, tm, Q_LORA), lambda bi, i: (bi, i, 0)),
            pl.BlockSpec(w_q.shape, lambda bi, i: (0, 0)),
            pl.BlockSpec((tm, QK_ROPE), lambda bi, i: (i, 0)),
            pl.BlockSpec((tm, QK_ROPE), lambda bi, i: (i, 0)),
        ],
        out_specs=pl.BlockSpec((1, N_HEADS, tm, QK_DIM), lambda bi, i: (bi, 0, i, 0)),
        compiler_params=_cparams("arbitrary", "arbitrary"),
        name="q_up",
    )(cq, w_q, cos, sin)


def _kv_up_kernel(ckv_ref, kr_ref, w_ref, k_ref, v_ref):
    ckv = ckv_ref[0]
    kr = kr_ref[0]
    for h in range(N_HEADS):
        p = jnp.dot(ckv, w_ref[:, h * KV_HEAD_COLS:(h + 1) * KV_HEAD_COLS],
                    preferred_element_type=F32)
        k_ref[0, h, :, :QK_NOPE] = p[:, :QK_NOPE].astype(BF16)
        k_ref[0, h, :, QK_NOPE:] = kr
        v_ref[0, h] = p[:, QK_NOPE:].astype(BF16)


def _kv_up(ckv, kr, w_kv, tm):
    b, l, _ = ckv.shape
    return pl.pallas_call(
        _kv_up_kernel,
        out_shape=(jax.ShapeDtypeStruct((b, N_HEADS, l, QK_DIM), BF16),
                   jax.ShapeDtypeStruct((b, N_HEADS, l, V_DIM), BF16)),
        grid=(b, l // tm),
        in_specs=[
            pl.BlockSpec((1, tm, KV_LORA), lambda bi, i: (bi, i, 0)),
            pl.BlockSpec((1, tm, QK_ROPE), lambda bi, i: (bi, i, 0)),
            pl.BlockSpec(w_kv.shape, lambda bi, i: (0, 0)),
        ],
        out_specs=(pl.BlockSpec((1, N_HEADS, tm, QK_DIM), lambda bi, i: (bi, 0, i, 0)),
                   pl.BlockSpec((1, N_HEADS, tm, V_DIM), lambda bi, i: (bi, 0, i, 0))),
        compiler_params=_cparams("arbitrary", "arbitrary"),
        name="kv_up",
    )(ckv, kr, w_kv)


def _attn_kernel(q_ref, k_ref, v_ref, o_ref, *, tk, n_chunks):
    q = q_ref[0, 0]
    tq = q.shape[0]
    n_slabs = tk // LANES
    m = jnp.full((tq, LANES), -jnp.inf, F32)
    l_loc = jnp.zeros((tq, LANES), F32)
    acc = jnp.zeros((tq, V_DIM), F32)
    for c in range(n_chunks):
        k = k_ref[0, 0, c * tk:(c + 1) * tk, :]
        s = lax.dot_general(q, k, (((1,), (1,)), ((), ())), preferred_element_type=F32)
        slabs = [s[:, j * LANES:(j + 1) * LANES] for j in range(n_slabs)]
        m_loc = slabs[0]
        for slab in slabs[1:]:
            m_loc = jnp.maximum(m_loc, slab)
        m_new = jnp.maximum(m, jnp.max(m_loc, axis=-1, keepdims=True))
        alpha = jnp.exp(m - m_new)
        ps = [jnp.exp(slab - m_new) for slab in slabs]
        p_sum = ps[0]
        for p in ps[1:]:
            p_sum = p_sum + p
        l_loc = alpha * l_loc + p_sum
        p_bf = jnp.concatenate([p.astype(BF16) for p in ps], axis=-1)
        acc = alpha * acc + jnp.dot(p_bf, v_ref[0, 0, c * tk:(c + 1) * tk, :],
                                    preferred_element_type=F32)
        m = m_new
    o_ref[0] = (acc / jnp.sum(l_loc, axis=-1, keepdims=True)).astype(BF16)


def _attention(q, k, v, tq, tk):
    b, h, l, _ = q.shape
    lk = k.shape[2]
    return pl.pallas_call(
        functools.partial(_attn_kernel, tk=tk, n_chunks=lk // tk),
        out_shape=jax.ShapeDtypeStruct((b, l, h * V_DIM), BF16),
        grid=(b, h, l // tq),
        in_specs=[
            pl.BlockSpec((1, 1, tq, QK_DIM), lambda bi, hi, i: (bi, hi, i, 0)),
            pl.BlockSpec((1, 1, lk, QK_DIM), lambda bi, hi, i: (bi, hi, 0, 0)),
            pl.BlockSpec((1, 1, lk, V_DIM), lambda bi, hi, i: (bi, hi, 0, 0)),
        ],
        out_specs=pl.BlockSpec((1, tq, V_DIM), lambda bi, hi, i: (bi, i, hi)),
        compiler_params=_cparams("arbitrary", "arbitrary", "arbitrary"),
        name="attention",
    )(q, k, v)


def _proj_res_kernel(o_ref, w_ref, x_ref, g1_ref, out_ref):
    mix = jnp.dot(o_ref[0], w_ref[...], preferred_element_type=F32)
    out_ref[0] = x_ref[0] + g1_ref[0] * mix


def _proj_res(o, w_out, x, mod3, gate_chunk):
    b, l, d = x.shape
    tm = 512
    row = lambda bi, i: (bi, i, 0)
    return pl.pallas_call(
        _proj_res_kernel,
        out_shape=jax.ShapeDtypeStruct((b, l, d), F32),
        grid=(b, l // tm),
        in_specs=[
            pl.BlockSpec((1, tm, o.shape[2]), row),
            pl.BlockSpec(w_out.shape, lambda bi, i: (0, 0)),
            pl.BlockSpec((1, tm, d), row),
            pl.BlockSpec((1, 1, d), lambda bi, i: (bi, 0, gate_chunk)),
        ],
        out_specs=pl.BlockSpec((1, tm, d), row),
        compiler_params=_cparams("arbitrary", "arbitrary"),
        name="proj_res",
    )(o, w_out, x, mod3)


def _conv_in_kernel(x_ref, g_ref, sh_ref, sc_ref, wb_ref, wc_ref, wu_ref, gb_ref, z_ref, h_ref):
    @pl.when(pl.program_id(2) == 0)
    def _():
        h_ref[...] = _modulate(x_ref[0], g_ref[...], sh_ref[0], sc_ref[0]).astype(BF16)

    h = h_ref[...]
    gb_ref[0] = jnp.dot(h, wb_ref[...], preferred_element_type=F32).astype(BF16)
    gc = jnp.dot(h, wc_ref[...], preferred_element_type=F32)
    u = jnp.dot(h, wu_ref[...], preferred_element_type=F32)
    z_ref[0] = (gc * u).astype(BF16)


def _conv_in(x, g, mod3, chunk0, w_in):
    b, l, d = x.shape
    tm, tn = 512, 512
    nn = d // tn
    row = lambda bi, i, j: (bi, i, 0)
    col = lambda bi, i, j: (bi, i, j)
    return pl.pallas_call(
        _conv_in_kernel,
        out_shape=(jax.ShapeDtypeStruct((b, l, d), BF16), jax.ShapeDtypeStruct((b, l, d), BF16)),
        grid=(b, l // tm, nn),
        in_specs=[
            pl.BlockSpec((1, tm, d), row),
            pl.BlockSpec((1, d), lambda bi, i, j: (0, 0)),
            pl.BlockSpec((1, 1, d), lambda bi, i, j: (bi, 0, chunk0)),
            pl.BlockSpec((1, 1, d), lambda bi, i, j: (bi, 0, chunk0 + 1)),
            pl.BlockSpec((d, tn), lambda bi, i, j: (0, j)),
            pl.BlockSpec((d, tn), lambda bi, i, j: (0, nn + j)),
            pl.BlockSpec((d, tn), lambda bi, i, j: (0, 2 * nn + j)),
        ],
        out_specs=(pl.BlockSpec((1, tm, tn), col), pl.BlockSpec((1, tm, tn), col)),
        scratch_shapes=[pltpu.VMEM((tm, d), BF16)],
        compiler_params=_cparams("arbitrary", "arbitrary", "arbitrary"),
        name="conv_in",
    )(x, g, mod3, mod3, w_in, w_in, w_in)


def _conv_out_kernel(z_ref, zp_ref, zn_ref, gb_ref, cw_ref, w_ref, x_ref, g1_ref, out_ref):
    i = pl.program_id(1)
    last = pl.num_programs(1) - 1
    z = z_ref[0].astype(F32)
    tm = z.shape[0]
    prev_row = jnp.where(i > 0, zp_ref[0, SUBLANES - 1:SUBLANES, :].astype(F32), 0.0)
    next_row = jnp.where(i < last, zn_ref[0, 0:1, :].astype(F32), 0.0)
    ridx = lax.broadcasted_iota(jnp.int32, z.shape, 0)
    z_prev = jnp.where(ridx == 0, prev_row, pltpu.roll(z, 1, 0))
    z_next = jnp.where(ridx == tm - 1, next_row, pltpu.roll(z, tm - 1, 0))
    conv = cw_ref[0:1, :] * z_prev + cw_ref[1:2, :] * z + cw_ref[2:3, :] * z_next
    y = (gb_ref[0].astype(F32) * conv).astype(BF16)
    mix = jnp.dot(y, w_ref[...], preferred_element_type=F32)
    out_ref[0] = x_ref[0] + g1_ref[0] * mix


def _conv_out(z, gb, conv_w, w_out, x, mod3, gate_chunk):
    b, l, d = x.shape
    tm = 256
    hb = tm // SUBLANES
    n_halo = l // SUBLANES
    row = lambda bi, i: (bi, i, 0)
    return pl.pallas_call(
        _conv_out_kernel,
        out_shape=jax.ShapeDtypeStruct((b, l, d), F32),
        grid=(b, l // tm),
        in_specs=[
            pl.BlockSpec((1, tm, d), row),
            pl.BlockSpec((1, SUBLANES, d), lambda bi, i: (bi, jnp.maximum(i * hb - 1, 0), 0)),
            pl.BlockSpec((1, SUBLANES, d),
                         lambda bi, i: (bi, jnp.minimum((i + 1) * hb, n_halo - 1), 0)),
            pl.BlockSpec((1, tm, d), row),
            pl.BlockSpec(conv_w.shape, lambda bi, i: (0, 0)),
            pl.BlockSpec(w_out.shape, lambda bi, i: (0, 0)),
            pl.BlockSpec((1, tm, d), row),
            pl.BlockSpec((1, 1, d), lambda bi, i: (bi, 0, gate_chunk)),
        ],
        out_specs=pl.BlockSpec((1, tm, d), row),
        compiler_params=_cparams("arbitrary", "arbitrary"),
        name="conv_out",
    )(z, z, z, gb, conv_w, w_out, x, mod3)


def _pack_rows(h, hp_ref):
    tm, d = h.shape
    bits = pltpu.bitcast(h.astype(BF16).astype(F32), jnp.uint32)
    words = (bits[:, :d // 2] >> 16) | (bits[:, d // 2:] & jnp.uint32(0xFFFF0000))
    for s in range(ROW_WORDS):
        hp_ref[pl.ds(s, tm, stride=ROW_WORDS), :] = words[:, s * LANES:(s + 1) * LANES]


def _unpack_rows(buf_ref, n_rows):
    lows, highs = [], []
    for s in range(ROW_WORDS):
        w = buf_ref[pl.ds(s, n_rows, stride=ROW_WORDS), :]
        lows.append(pltpu.bitcast(w << 16, F32).astype(BF16))
        highs.append(pltpu.bitcast(w & jnp.uint32(0xFFFF0000), F32).astype(BF16))
    return jnp.concatenate(lows + highs, axis=-1)


def _router_kernel(x_ref, g_ref, sh_ref, sc_ref, wr_ref, br_ref, hp_ref, idx_ref, gate_ref):
    h = _modulate(x_ref[0], g_ref[...], sh_ref[0], sc_ref[0])
    _pack_rows(h, hp_ref)
    logits = jnp.dot(h, wr_ref[...], preferred_element_type=F32,
                     precision=lax.Precision.HIGHEST) + br_ref[...]
    lane = lax.broadcasted_iota(jnp.int32, logits.shape, 1).astype(F32)
    work = jnp.where(lane < N_EXPERTS, logits, -jnp.inf)
    vals, idxs = [], []
    for _ in range(TOP_K):
        mx = jnp.max(work, axis=-1, keepdims=True)
        ix = jnp.min(jnp.where(work == mx, lane, float(LANES)), axis=-1, keepdims=True)
        vals.append(mx)
        idxs.append(ix.astype(jnp.int32))
        work = jnp.where(lane == ix, -jnp.inf, work)
    ex = [jnp.exp(v - vals[0]) for v in vals]
    den = ex[0] + ex[1] + ex[2] + ex[3]
    for k in range(TOP_K):
        idx_ref[:, k:k + 1] = idxs[k]
        gate_ref[:, k:k + 1] = ex[k] / den


def _router(x, g, mod3, chunk0, w_r, b_r):
    b, l, d = x.shape
    tm = 512
    nt = l // tm
    t = b * l
    return pl.pallas_call(
        _router_kernel,
        out_shape=(jax.ShapeDtypeStruct((t * ROW_WORDS, LANES), jnp.uint32),
                   jax.ShapeDtypeStruct((t, TOP_K), jnp.int32),
                   jax.ShapeDtypeStruct((t, TOP_K), F32)),
        grid=(b, nt),
        in_specs=[
            pl.BlockSpec((1, tm, d), lambda bi, i: (bi, i, 0)),
            pl.BlockSpec((1, d), lambda bi, i: (0, 0)),
            pl.BlockSpec((1, 1, d), lambda bi, i: (bi, 0, chunk0)),
            pl.BlockSpec((1, 1, d), lambda bi, i: (bi, 0, chunk0 + 1)),
            pl.BlockSpec((d, LANES), lambda bi, i: (0, 0)),
            pl.BlockSpec((1, LANES), lambda bi, i: (0, 0)),
        ],
        out_specs=(pl.BlockSpec((tm * ROW_WORDS, LANES), lambda bi, i: (bi * nt + i, 0)),
                   pl.BlockSpec((tm, TOP_K), lambda bi, i: (bi * nt + i, 0)),
                   pl.BlockSpec((tm, TOP_K), lambda bi, i: (bi * nt + i, 0))),
        compiler_params=_cparams("arbitrary", "arbitrary"),
        name="router",
    )(x, g, mod3, mod3, w_r, b_r)


def _issue_row_copies(n_rows, make_copy):
    def issue(it, carry):
        for u in range(DMA_UNROLL):
            make_copy(it * DMA_UNROLL + u).start(priority=u % 2)
        return carry

    lax.fori_loop(0, n_rows // DMA_UNROLL, issue, 0)


def _gather_kernel(nu_ref, tok_ref, tokn_ref, hp_ref, xs_ref, buf_ref, sem):
    i = pl.program_id(0)
    n_used = nu_ref[0]
    slot = lax.rem(i, 2)

    def issue_from(t_ref, dst_slot):
        def make_copy(r):
            src = pl.multiple_of(t_ref[0, 0, r] * ROW_WORDS, ROW_WORDS)
            return pltpu.make_async_copy(hp_ref.at[pl.ds(src, ROW_WORDS)],
                                         buf_ref.at[dst_slot, pl.ds(r * ROW_WORDS, ROW_WORDS)],
                                         sem.at[dst_slot])
        _issue_row_copies(GATHER_ROWS, make_copy)

    @pl.when(jnp.logical_and(i == 0, n_used > 0))
    def _():
        issue_from(tok_ref, 0)

    @pl.when(i + 1 < n_used)
    def _():
        issue_from(tokn_ref, 1 - slot)

    @pl.when(i < n_used)
    def _():
        pltpu.make_async_copy(hp_ref.at[pl.ds(0, GATHER_ROWS * ROW_WORDS)], buf_ref.at[slot],
                              sem.at[slot]).wait()
        xs_ref[...] = _unpack_rows(buf_ref.at[slot], GATHER_ROWS)

    @pl.when(i >= n_used)
    def _():
        xs_ref[...] = jnp.zeros_like(xs_ref)


def _gather_rows(n_used, slot_tok, hp, d):
    n_slots = slot_tok.shape[0]
    n_steps = n_slots // GATHER_ROWS
    tok3 = slot_tok.reshape(n_steps, 1, GATHER_ROWS)
    return pl.pallas_call(
        _gather_kernel,
        out_shape=jax.ShapeDtypeStruct((n_slots, d), BF16),
        grid_spec=pltpu.PrefetchScalarGridSpec(
            num_scalar_prefetch=1,
            grid=(n_steps,),
            in_specs=[
                pl.BlockSpec((1, 1, GATHER_ROWS), lambda i, nu: (i, 0, 0), memory_space=pltpu.SMEM),
                pl.BlockSpec((1, 1, GATHER_ROWS),
                             lambda i, nu: (jnp.minimum(i + 1, n_steps - 1), 0, 0),
                             memory_space=pltpu.SMEM),
                pl.BlockSpec(memory_space=pl.ANY),
            ],
            out_specs=pl.BlockSpec((GATHER_ROWS, d), lambda i, nu: (i, 0)),
            scratch_shapes=[pltpu.VMEM((2, GATHER_ROWS * ROW_WORDS, LANES), jnp.uint32),
                            pltpu.SemaphoreType.DMA((2,))],
        ),
        compiler_params=_cparams("arbitrary"),
        name="gather_rows",
    )(n_used, tok3, tok3, hp)


def _block_state(be_ref):
    bi = pl.program_id(1)
    used = bi < be_ref[pl.num_programs(1)]
    prev = be_ref[jnp.maximum(bi - 1, 0)]
    return used, jnp.logical_or(bi == 0, be_ref[bi] != prev)


def _gmm_gu_kernel(be_ref, x_ref, wg_ref, wl_ref, bg_ref, bl_ref, act_ref, wg_s, wl_s):
    used, changed = _block_state(be_ref)

    @pl.when(jnp.logical_and(used, changed))
    def _():
        wg_s[...] = wg_ref[...].astype(BF16)
        wl_s[...] = wl_ref[...].astype(BF16)

    @pl.when(used)
    def _():
        xb = x_ref[...]
        glu = jnp.dot(xb, wg_s[...], preferred_element_type=F32) + bg_ref[...]
        lin = jnp.dot(xb, wl_s[...], preferred_element_type=F32) + bl_ref[...]
        glu = jnp.minimum(glu, SWIGLU_LIMIT)
        lin = jnp.clip(lin, -SWIGLU_LIMIT, SWIGLU_LIMIT)
        sig = 1.0 / (1.0 + jnp.exp(-SWIGLU_ALPHA * glu))
        act_ref[...] = (glu * sig * (lin + 1.0)).astype(BF16)

    @pl.when(jnp.logical_not(used))
    def _():
        act_ref[...] = jnp.zeros_like(act_ref)


def _gmm_gu(block_e, xs, w_gu, b_gu4, layer):
    n_slots, d = xs.shape
    f = w_gu.shape[3] // 2
    tm, tn = EXPERT_TM, 1024
    nn = f // tn
    return pl.pallas_call(
        _gmm_gu_kernel,
        out_shape=jax.ShapeDtypeStruct((n_slots, f), BF16),
        grid_spec=pltpu.PrefetchScalarGridSpec(
            num_scalar_prefetch=1,
            grid=(nn, n_slots // tm),
            in_specs=[
                pl.BlockSpec((tm, d), lambda j, i, be: (i, 0)),
                pl.BlockSpec((None, None, d, tn), lambda j, i, be: (layer, be[i], 0, j)),
                pl.BlockSpec((None, None, d, tn), lambda j, i, be: (layer, be[i], 0, nn + j)),
                pl.BlockSpec((None, None, 1, tn), lambda j, i, be: (layer, be[i], 0, j)),
                pl.BlockSpec((None, None, 1, tn), lambda j, i, be: (layer, be[i], 0, nn + j)),
            ],
            out_specs=pl.BlockSpec((tm, tn), lambda j, i, be: (i, j)),
            scratch_shapes=[pltpu.VMEM((d, tn), BF16), pltpu.VMEM((d, tn), BF16)],
        ),
        compiler_params=_cparams("arbitrary", "arbitrary"),
        name="gmm_gate_up",
    )(block_e, xs, w_gu, w_gu, b_gu4, b_gu4)


DOWN_TN = SUBLANES * LANES


def _gmm_down_kernel(be_ref, a_ref, w_ref, b_ref, y_ref, w_s):
    used, changed = _block_state(be_ref)

    @pl.when(jnp.logical_and(used, changed))
    def _():
        w_s[...] = w_ref[...].astype(BF16)

    @pl.when(used)
    def _():
        y = jnp.dot(a_ref[...], w_s[...], preferred_element_type=F32) + b_ref[...]
        tm = y.shape[0]
        y2_ref = y_ref.reshape(tm * SUBLANES, LANES)
        for cc in range(SUBLANES):
            y2_ref[pl.ds(cc, tm, stride=SUBLANES), :] = y[:, cc * LANES:(cc + 1) * LANES]

    @pl.when(jnp.logical_not(used))
    def _():
        y_ref[...] = jnp.zeros_like(y_ref)


def _gmm_down(block_e, act, w_down, b_down4, layer):
    n_slots, f = act.shape
    d = w_down.shape[3]
    tm, tn = EXPERT_TM, DOWN_TN
    return pl.pallas_call(
        _gmm_down_kernel,
        out_shape=jax.ShapeDtypeStruct((n_slots, d // LANES, LANES), F32),
        grid_spec=pltpu.PrefetchScalarGridSpec(
            num_scalar_prefetch=1,
            grid=(d // tn, n_slots // tm),
            in_specs=[
                pl.BlockSpec((tm, f), lambda j, i, be: (i, 0)),
                pl.BlockSpec((None, None, f, tn), lambda j, i, be: (layer, be[i], 0, j)),
                pl.BlockSpec((None, None, 1, tn), lambda j, i, be: (layer, be[i], 0, j)),
            ],
            out_specs=pl.BlockSpec((tm, SUBLANES, LANES), lambda j, i, be: (i, j, 0)),
            scratch_shapes=[pltpu.VMEM((f, tn), BF16)],
        ),
        compiler_params=_cparams("arbitrary", "arbitrary"),
        name="gmm_down",
    )(block_e, act, w_down, b_down4)


COMBINE_ROWS = COMBINE_TOK * TOP_K


def _combine_kernel(pos_ref, posn_ref, ys_ref, gate_ref, x_ref, g2_ref, gf_ref, out_ref,
                    buf_ref, sem, *, final_norm):
    i = pl.program_id(0)
    n = pl.num_programs(0)
    slot = lax.rem(i, 2)
    n_chunks = x_ref.shape[1] // LANES

    def issue_from(p_ref, dst_slot):
        def make_copy(r):
            src = pl.multiple_of(p_ref[0, 0, r] * n_chunks, n_chunks)
            return pltpu.make_async_copy(ys_ref.at[pl.ds(src, n_chunks)],
                                         buf_ref.at[dst_slot, pl.ds(r * n_chunks, n_chunks)],
                                         sem.at[dst_slot])
        _issue_row_copies(COMBINE_ROWS, make_copy)

    @pl.when(i == 0)
    def _():
        issue_from(pos_ref, 0)

    @pl.when(i + 1 < n)
    def _():
        issue_from(posn_ref, 1 - slot)

    pltpu.make_async_copy(ys_ref.at[pl.ds(0, COMBINE_ROWS * n_chunks)], buf_ref.at[slot],
                          sem.at[slot]).wait()

    rows_ref = buf_ref.at[slot]
    gates = [jnp.broadcast_to(gate_ref[:, k:k + 1], (COMBINE_TOK, LANES)) for k in range(TOP_K)]
    ssq = jnp.zeros((COMBINE_TOK, LANES), F32)
    for c in range(n_chunks):
        cols = slice(c * LANES, (c + 1) * LANES)
        y = None
        for k in range(TOP_K):
            rows = rows_ref[pl.ds(k * COMBINE_TOK * n_chunks + c, COMBINE_TOK, stride=n_chunks), :]
            y = gates[k] * rows if y is None else y + gates[k] * rows
        o = x_ref[:, cols] + g2_ref[0, :, cols] * y
        out_ref[:, cols] = o
        ssq = ssq + o * o
    if final_norm:
        d = n_chunks * LANES
        inv = lax.rsqrt(jnp.sum(ssq, axis=-1, keepdims=True) / d + RMS_EPS)
        out_ref[...] = out_ref[...] * inv * gf_ref[...]


def _combine(pos, ys2, gate, x2, mod3, gate_chunk, tokens_per_batch, final_g, final_norm):
    t, d = x2.shape
    n_chunks = d // LANES
    n_steps = t // COMBINE_TOK
    steps_per_batch = tokens_per_batch // COMBINE_TOK
    pos3 = pos.reshape(n_steps, COMBINE_TOK, TOP_K).transpose(0, 2, 1).reshape(n_steps, 1, COMBINE_ROWS)
    return pl.pallas_call(
        functools.partial(_combine_kernel, final_norm=final_norm),
        out_shape=jax.ShapeDtypeStruct((t, d), F32),
        grid=(n_steps,),
        in_specs=[
            pl.BlockSpec((1, 1, COMBINE_ROWS), lambda i: (i, 0, 0), memory_space=pltpu.SMEM),
            pl.BlockSpec((1, 1, COMBINE_ROWS), lambda i: (jnp.minimum(i + 1, n_steps - 1), 0, 0),
                         memory_space=pltpu.SMEM),
            pl.BlockSpec(memory_space=pl.ANY),
            pl.BlockSpec((COMBINE_TOK, TOP_K), lambda i: (i, 0)),
            pl.BlockSpec((COMBINE_TOK, d), lambda i: (i, 0)),
            pl.BlockSpec((1, 1, d), lambda i: (i // steps_per_batch, 0, gate_chunk)),
            pl.BlockSpec((1, d), lambda i: (0, 0)),
        ],
        out_specs=pl.BlockSpec((COMBINE_TOK, d), lambda i: (i, 0)),
        scratch_shapes=[pltpu.VMEM((2, COMBINE_ROWS * n_chunks, LANES), F32),
                        pltpu.SemaphoreType.DMA((2,))],
        compiler_params=_cparams("arbitrary"),
        name="combine",
    )(pos3, pos3, ys2, gate, x2, mod3, final_g)


def _slot_plan(top_idx):
    t = top_idx.shape[0]
    n_assign = t * TOP_K
    flat_e = top_idx.reshape(-1)
    onehot = (flat_e[:, None] == jnp.arange(N_EXPERTS, dtype=jnp.int32)[None, :]).astype(jnp.int32)
    csum = jnp.cumsum(onehot, axis=0)
    counts = csum[-1]
    rank = jnp.sum(csum * onehot, axis=1) - 1
    padded = (counts + EXPERT_TM - 1) // EXPERT_TM * EXPERT_TM
    padded_end = jnp.cumsum(padded)
    padded_start = padded_end - padded
    pos = (padded_start[flat_e] + rank).astype(jnp.int32)
    n_blocks = -(-(n_assign + N_EXPERTS * (EXPERT_TM - 1)) // EXPERT_TM)
    n_blocks = -(-n_blocks * EXPERT_TM // GATHER_ROWS) * GATHER_ROWS // EXPERT_TM
    n_slots = n_blocks * EXPERT_TM
    flat_tok = jnp.arange(n_assign, dtype=jnp.int32) // TOP_K
    slot_tok = jnp.zeros((n_slots,), jnp.int32).at[pos].set(
        flat_tok, unique_indices=True, mode="promise_in_bounds")
    block_start = jnp.arange(n_blocks, dtype=jnp.int32) * EXPERT_TM
    block_e = jnp.minimum(jnp.sum(block_start[:, None] >= padded_end[None, :], axis=1),
                          N_EXPERTS - 1).astype(jnp.int32)
    n_used = (padded_end[-1:] // EXPERT_TM).astype(jnp.int32)
    return pos, slot_tok, jnp.concatenate([block_e, n_used]), n_used


def _expert_ffn_residual(x, mod3, layer, norm_g, w_r, b_r, w_gu, b_gu4, w_down, b_down4,
                         final_g, final_norm):
    b, l, d = x.shape
    assert d == 2 * ROW_WORDS * LANES
    hp, top_idx, gate = _router(x, norm_g, mod3, 3, w_r, b_r)
    pos, slot_tok, block_e, n_used = _slot_plan(top_idx)
    xs = _gather_rows(n_used, slot_tok, hp, d)
    act = _gmm_gu(block_e, xs, w_gu, b_gu4, layer)
    ys3 = _gmm_down(block_e, act, w_down, b_down4, layer)
    ys2 = ys3.reshape(ys3.shape[0] * ys3.shape[1], LANES)
    out = _combine(pos, ys2, gate, x.reshape(b * l, d), mod3, 5, l, final_g, final_norm)
    return out.reshape(b, l, d)


def _rope_tables(length):
    rows = length // GRID_W
    row = jnp.repeat(jnp.arange(rows), GRID_W).astype(F32)
    col = jnp.tile(jnp.arange(GRID_W), rows).astype(F32)
    inv = 1.0 / (ROPE_THETA ** (jnp.arange(0, ROPE_AXIS, 2, dtype=F32) / ROPE_AXIS))
    ang_r = row[:, None] * inv[None, :]
    ang_c = col[:, None] * inv[None, :]
    cos = jnp.concatenate([jnp.cos(ang_r)] * 2 + [jnp.cos(ang_c)] * 2, axis=-1)
    sin = jnp.concatenate([jnp.sin(ang_r)] * 2 + [jnp.sin(ang_c)] * 2, axis=-1)
    return cos, sin


def _rotate_half_cols(w):
    a, b_, c_, d_ = jnp.split(w, 4, axis=-1)
    return jnp.concatenate([-b_, a, -d_, c_], axis=-1)


def kernel(x, c, ctx, c_ctx, ada_w, ada_b, norm_mix_g, norm_ffn_g, mla_w_in, mla_q_norm_g,
           mla_kv_norm_g, mla_w_q_up, mla_w_kv_up, mla_w_out, conv_w_in, conv_w, conv_w_out,
           router_w, router_b, expert_w_gu, expert_b_gu, expert_w_down, expert_b_down,
           final_norm_g):
    b, l, d = x.shape
    depth = ada_w.shape[0]
    lc = ctx.shape[1]
    assert depth == 2 and b + 1 <= ADA_ROWS

    cv = jnp.concatenate([c, c_ctx[None, :], jnp.zeros((ADA_ROWS - b - 1, d), F32)], axis=0)
    ctx_row = b
    ada_b3 = ada_b.reshape(depth, 1, N_ADA * d)
    cos, sin = _rope_tables(l)
    b_gu4 = expert_b_gu.reshape(depth, N_EXPERTS, 1, -1)
    b_down4 = expert_b_down.reshape(depth, N_EXPERTS, 1, d)
    w_r = jnp.pad(router_w, ((0, 0), (0, 0), (0, LANES - N_EXPERTS)))
    b_r = jnp.pad(router_b, ((0, 0), (0, LANES - N_EXPERTS)))[:, None, :]
    final_g = final_norm_g[None, :]

    mod3 = _ada(cv, ada_w, ada_b3, 0).reshape(ADA_ROWS, 1, N_ADA * d)
    w_in = mla_w_in[0]
    w_kr = w_in[:, Q_LORA + KV_LORA:]
    w_all = jnp.concatenate([w_in, _rotate_half_cols(w_kr)], axis=1).astype(BF16)
    w_ctx = w_in[:, Q_LORA:].astype(BF16)
    gq = mla_q_norm_g[0][None, :]
    gkv = mla_kv_norm_g[0][None, :]
    g_mix = norm_mix_g[0][None, :]
    wq = mla_w_q_up[0].reshape(Q_LORA, N_HEADS, QK_DIM)
    wq_rope = wq[..., QK_NOPE:]
    wq_all = jnp.concatenate([wq, _rotate_half_cols(wq_rope)], axis=-1)
    wq_all = wq_all.reshape(Q_LORA, N_HEADS * Q_HEAD_COLS).astype(BF16)
    wkv = mla_w_kv_up[0].astype(BF16)
    w_o = mla_w_out[0].astype(BF16)

    cq, ckv, kr = _mla_in(x, g_mix, mod3, 0, w_all, gq, gkv, cos, sin)
    ckv_c, kr_c = _mla_in_ctx(ctx, g_mix, mod3, ctx_row, w_ctx, gkv)
    ckv_all = jnp.concatenate([ckv_c, ckv], axis=1)
    kr_all = jnp.concatenate([kr_c, kr], axis=1)
    q = _q_up(cq, wq_all, cos, sin)
    k, v = _kv_up(ckv_all, kr_all, wkv, 768)
    o = _attention(q, k, v, 1024, 768)
    x = _proj_res(o, w_o, x, mod3, 2)
    x = _expert_ffn_residual(x, mod3, 0, norm_ffn_g[0][None, :], w_r[0], b_r[0], expert_w_gu,
                             b_gu4, expert_w_down, b_down4, final_g, False)

    mod3 = _ada(cv, ada_w, ada_b3, 1).reshape(ADA_ROWS, 1, N_ADA * d)
    gb, z = _conv_in(x, norm_mix_g[1][None, :], mod3, 0, conv_w_in[0].astype(BF16))
    x = _conv_out(z, gb, conv_w[0], conv_w_out[0].astype(BF16), x, mod3, 2)
    x = _expert_ffn_residual(x, mod3, 1, norm_ffn_g[1][None, :], w_r[1], b_r[1], expert_w_gu,
                             b_gu4, expert_w_down, b_down4, final_g, True)
    return x
```

```python
import functools

import jax
import jax.numpy as jnp
import numpy as np
from jax import lax
from jax.experimental import pallas as pl
from jax.experimental.pallas import tpu as pltpu

GRID_W = 64
RMS_EPS = 1e-6
N_ADA = 6
N_HEADS = 16
Q_LORA = 512
KV_LORA = 512
QK_NOPE = 128
QK_ROPE = 64
V_DIM = 128
ROPE_AXIS = QK_ROPE // 2
ROPE_THETA = 10000.0
QK_DIM = QK_NOPE + QK_ROPE
ATTN_SCALE = QK_DIM ** -0.5
LOG2_E = 1.4426950408889634
Q_SCALE = ATTN_SCALE * LOG2_E
N_EXPERTS = 32
TOP_K = 4
SWIGLU_LIMIT = 7.0
SWIGLU_ALPHA = 1.702

LANES = 128
SUBLANES = 8
VMEM_LIMIT = 56 * 1024 * 1024

ADA_ROWS = 8
EXPERT_TM = 512
GATHER_ROWS = EXPERT_TM
COMBINE_TOK = 128
ROW_WORDS = SUBLANES
DMA_UNROLL = 8

BF16 = jnp.bfloat16
F32 = jnp.float32


def _cparams(*sem):
    return pltpu.CompilerParams(dimension_semantics=sem, vmem_limit_bytes=VMEM_LIMIT)


def _rms(x):
    return x * lax.rsqrt(jnp.mean(x * x, axis=-1, keepdims=True) + RMS_EPS)


def _modulate(x, g, shift, scale):
    return _rms(x) * g * (1.0 + scale) + shift


def _ada_kernel(cv_ref, w_ref, b_ref, o_ref):
    cv = cv_ref[...]
    s = cv * (1.0 / (1.0 + jnp.exp(-cv)))
    o_ref[...] = jnp.dot(s, w_ref[...], preferred_element_type=F32,
                         precision=lax.Precision.HIGHEST) + b_ref[...]


def _ada(cv, ada_w, ada_b3, layer):
    d = cv.shape[1]
    n = ada_w.shape[2]
    tn = 1024
    return pl.pallas_call(
        _ada_kernel,
        out_shape=jax.ShapeDtypeStruct((ADA_ROWS, n), F32),
        grid=(n // tn,),
        in_specs=[
            pl.BlockSpec((ADA_ROWS, d), lambda j: (0, 0)),
            pl.BlockSpec((None, d, tn), lambda j: (layer, 0, j)),
            pl.BlockSpec((None, 1, tn), lambda j: (layer, 0, j)),
        ],
        out_specs=pl.BlockSpec((ADA_ROWS, tn), lambda j: (0, j)),
        compiler_params=_cparams("arbitrary"),
        name="ada",
    )(cv, ada_w, ada_b3)


def _mla_in_kernel(x_ref, g_ref, sh_ref, sc_ref, w_ref, gq_ref, gkv_ref, cos_ref, sin_ref,
                   cq_ref, ckv_ref, kr_ref):
    h = _modulate(x_ref[0], g_ref[...], sh_ref[0], sc_ref[0]).astype(BF16)
    p = jnp.dot(h, w_ref[...], preferred_element_type=F32)
    cq_ref[0] = (_rms(p[:, :Q_LORA]) * gq_ref[...]).astype(BF16)
    ckv_ref[0] = (_rms(p[:, Q_LORA:Q_LORA + KV_LORA]) * gkv_ref[...]).astype(BF16)
    o = Q_LORA + KV_LORA
    kr = p[:, o:o + QK_ROPE]
    kr_rot = p[:, o + QK_ROPE:o + 2 * QK_ROPE]
    kr_ref[0] = (kr * cos_ref[...] + kr_rot * sin_ref[...]).astype(BF16)


def _mla_in_ctx_kernel(x_ref, g_ref, sh_ref, sc_ref, w_ref, gkv_ref, ckv_ref, kr_ref):
    h = _modulate(x_ref[0], g_ref[...], sh_ref[0], sc_ref[0]).astype(BF16)
    p = jnp.dot(h, w_ref[...], preferred_element_type=F32)
    ckv_ref[0] = (_rms(p[:, :KV_LORA]) * gkv_ref[...]).astype(BF16)
    kr_ref[0] = p[:, KV_LORA:KV_LORA + QK_ROPE].astype(BF16)


def _mla_in(x, g, mod3, layer_chunk0, w_all, gq, gkv, cos, sin):
    b, l, d = x.shape
    tm = 512
    nw = w_all.shape[1]
    row = lambda bi, i: (bi, i, 0)
    return pl.pallas_call(
        _mla_in_kernel,
        out_shape=(jax.ShapeDtypeStruct((b, l, Q_LORA), BF16),
                   jax.ShapeDtypeStruct((b, l, KV_LORA), BF16),
                   jax.ShapeDtypeStruct((b, l, QK_ROPE), BF16)),
        grid=(b, l // tm),
        in_specs=[
            pl.BlockSpec((1, tm, d), row),
            pl.BlockSpec((1, d), lambda bi, i: (0, 0)),
            pl.BlockSpec((1, 1, d), lambda bi, i: (bi, 0, layer_chunk0)),
            pl.BlockSpec((1, 1, d), lambda bi, i: (bi, 0, layer_chunk0 + 1)),
            pl.BlockSpec((d, nw), lambda bi, i: (0, 0)),
            pl.BlockSpec((1, Q_LORA), lambda bi, i: (0, 0)),
            pl.BlockSpec((1, KV_LORA), lambda bi, i: (0, 0)),
            pl.BlockSpec((tm, QK_ROPE), lambda bi, i: (i, 0)),
            pl.BlockSpec((tm, QK_ROPE), lambda bi, i: (i, 0)),
        ],
        out_specs=(pl.BlockSpec((1, tm, Q_LORA), row),
                   pl.BlockSpec((1, tm, KV_LORA), row),
                   pl.BlockSpec((1, tm, QK_ROPE), row)),
        compiler_params=_cparams("arbitrary", "arbitrary"),
        name="mla_in",
    )(x, g, mod3, mod3, w_all, gq, gkv, cos, sin)


def _mla_in_ctx(ctx, g, mod3, ctx_row, w_kv, gkv):
    b, l, d = ctx.shape
    tm = l
    nw = w_kv.shape[1]
    row = lambda bi, i: (bi, i, 0)
    return pl.pallas_call(
        _mla_in_ctx_kernel,
        out_shape=(jax.ShapeDtypeStruct((b, l, KV_LORA), BF16),
                   jax.ShapeDtypeStruct((b, l, QK_ROPE), BF16)),
        grid=(b, l // tm),
        in_specs=[
            pl.BlockSpec((1, tm, d), row),
            pl.BlockSpec((1, d), lambda bi, i: (0, 0)),
            pl.BlockSpec((1, 1, d), lambda bi, i: (ctx_row, 0, 0)),
            pl.BlockSpec((1, 1, d), lambda bi, i: (ctx_row, 0, 1)),
            pl.BlockSpec((d, nw), lambda bi, i: (0, 0)),
            pl.BlockSpec((1, KV_LORA), lambda bi, i: (0, 0)),
        ],
        out_specs=(pl.BlockSpec((1, tm, KV_LORA), row),
                   pl.BlockSpec((1, tm, QK_ROPE), row)),
        compiler_params=_cparams("arbitrary", "arbitrary"),
        name="mla_in_ctx",
    )(ctx, g, mod3, mod3, w_kv, gkv)


Q_HEAD_COLS = QK_NOPE + 2 * QK_ROPE
KV_HEAD_COLS = QK_NOPE + V_DIM


def _q_up_kernel(cq_ref, w_ref, cos_ref, sin_ref, q_ref):
    cq = cq_ref[0]
    cos = cos_ref[...]
    sin = sin_ref[...]
    for h in range(N_HEADS):
        p = jnp.dot(cq, w_ref[:, h * Q_HEAD_COLS:(h + 1) * Q_HEAD_COLS],
                    preferred_element_type=F32)
        rope = p[:, QK_NOPE:QK_NOPE + QK_ROPE] * cos + p[:, QK_NOPE + QK_ROPE:] * sin
        q_ref[0, h, :, :QK_NOPE] = (p[:, :QK_NOPE] * Q_SCALE).astype(BF16)
        q_ref[0, h, :, QK_NOPE:] = (rope * Q_SCALE).astype(BF16)


def _q_up(cq, w_q, cos, sin):
    b, l, _ = cq.shape
    tm = 512
    return pl.pallas_call(
        _q_up_kernel,
        out_shape=jax.ShapeDtypeStruct((b, N_HEADS, l, QK_DIM), BF16),
        grid=(b, l // tm),
        in_specs=[
            pl.BlockSpec((1, tm, Q_LORA), lambda bi, i: (bi, i, 0)),
            pl.BlockSpec(w_q.shape, lambda bi, i: (0, 0)),
            pl.BlockSpec((tm, QK_ROPE), lambda bi, i: (i, 0)),
            pl.BlockSpec((tm, QK_ROPE), lambda bi, i: (i, 0)),
        ],
        out_specs=pl.BlockSpec((1, N_HEADS, tm, QK_DIM), lambda bi, i: (bi, 0, i, 0)),
        compiler_params=_cparams("arbitrary", "arbitrary"),
        name="q_up",
    )(cq, w_q, cos, sin)


def _kv_up_kernel(ckv_ref, kr_ref, w_ref, k_ref, v_ref):
    ckv = ckv_ref[0]
    kr = kr_ref[0]
    for h in range(N_HEADS):
        p = jnp.dot(ckv, w_ref[:, h * KV_HEAD_COLS:(h + 1) * KV_HEAD_COLS],
                    preferred_element_type=F32)
        k_ref[0, h, :, :QK_NOPE] = p[:, :QK_NOPE].astype(BF16)
        k_ref[0, h, :, QK_NOPE:] = kr
        v_ref[0, h] = p[:, QK_NOPE:].astype(BF16)


def _kv_up(ckv, kr, w_kv, tm):
    b, l, _ = ckv.shape
    return pl.pallas_call(
        _kv_up_kernel,
        out_shape=(jax.ShapeDtypeStruct((b, N_HEADS, l, QK_DIM), BF16),
                   jax.ShapeDtypeStruct((b, N_HEADS, l, V_DIM), BF16)),
        grid=(b, l // tm),
        in_specs=[
            pl.BlockSpec((1, tm, KV_LORA), lambda bi, i: (bi, i, 0)),
            pl.BlockSpec((1, tm, QK_ROPE), lambda bi, i: (bi, i, 0)),
            pl.BlockSpec(w_kv.shape, lambda bi, i: (0, 0)),
        ],
        out_specs=(pl.BlockSpec((1, N_HEADS, tm, QK_DIM), lambda bi, i: (bi, 0, i, 0)),
                   pl.BlockSpec((1, N_HEADS, tm, V_DIM), lambda bi, i: (bi, 0, i, 0))),
        compiler_params=_cparams("arbitrary", "arbitrary"),
        name="kv_up",
    )(ckv, kr, w_kv)


def _attn_kernel(q_ref, k_ref, v_ref, o_ref, *, tk, n_chunks):
    q = q_ref[0, 0]
    tq = q.shape[0]
    n_slabs = tk // LANES
    m = jnp.full((tq, LANES), -jnp.inf, F32)
    l_loc = jnp.zeros((tq, LANES), F32)
    acc = jnp.zeros((tq, V_DIM), F32)
    for c in range(n_chunks):
        k = k_ref[0, 0, c * tk:(c + 1) * tk, :]
        s = lax.dot_general(q, k, (((1,), (1,)), ((), ())), preferred_element_type=F32)
        slabs = [s[:, j * LANES:(j + 1) * LANES] for j in range(n_slabs)]
        m_loc = slabs[0]
        for slab in slabs[1:]:
            m_loc = jnp.maximum(m_loc, slab)
        m_new = jnp.maximum(m, jnp.max(m_loc, axis=-1, keepdims=True))
        alpha = jnp.exp2(m - m_new)
        ps = [jnp.exp2(slab - m_new) for slab in slabs]
        p_sum = ps[0]
        for p in ps[1:]:
            p_sum = p_sum + p
        l_loc = alpha * l_loc + p_sum
        p_bf = jnp.concatenate([p.astype(BF16) for p in ps], axis=-1)
        acc = alpha * acc + jnp.dot(p_bf, v_ref[0, 0, c * tk:(c + 1) * tk, :],
                                    preferred_element_type=F32)
        m = m_new
    o_ref[0] = (acc / jnp.sum(l_loc, axis=-1, keepdims=True)).astype(BF16)


def _attention(q, k, v, tq, tk):
    b, h, l, _ = q.shape
    lk = k.shape[2]
    return pl.pallas_call(
        functools.partial(_attn_kernel, tk=tk, n_chunks=lk // tk),
        out_shape=jax.ShapeDtypeStruct((b, l, h * V_DIM), BF16),
        grid=(b, h, l // tq),
        in_specs=[
            pl.BlockSpec((1, 1, tq, QK_DIM), lambda bi, hi, i: (bi, hi, i, 0)),
            pl.BlockSpec((1, 1, lk, QK_DIM), lambda bi, hi, i: (bi, hi, 0, 0)),
            pl.BlockSpec((1, 1, lk, V_DIM), lambda bi, hi, i: (bi, hi, 0, 0)),
        ],
        out_specs=pl.BlockSpec((1, tq, V_DIM), lambda bi, hi, i: (bi, i, hi)),
        compiler_params=_cparams("arbitrary", "arbitrary", "arbitrary"),
        name="attention",
    )(q, k, v)


def _proj_res_kernel(o_ref, w_ref, x_ref, g1_ref, out_ref):
    mix = jnp.dot(o_ref[0], w_ref[...], preferred_element_type=F32)
    out_ref[0] = x_ref[0] + g1_ref[0] * mix


def _proj_res(o, w_out, x, mod3, gate_chunk):
    b, l, d = x.shape
    tm = 512
    row = lambda bi, i: (bi, i, 0)
    return pl.pallas_call(
        _proj_res_kernel,
        out_shape=jax.ShapeDtypeStruct((b, l, d), F32),
        grid=(b, l // tm),
        in_specs=[
            pl.BlockSpec((1, tm, o.shape[2]), row),
            pl.BlockSpec(w_out.shape, lambda bi, i: (0, 0)),
            pl.BlockSpec((1, tm, d), row),
            pl.BlockSpec((1, 1, d), lambda bi, i: (bi, 0, gate_chunk)),
        ],
        out_specs=pl.BlockSpec((1, tm, d), row),
        compiler_params=_cparams("arbitrary", "arbitrary"),
        name="proj_res",
    )(o, w_out, x, mod3)


def _conv_in_kernel(x_ref, g_ref, sh_ref, sc_ref, wb_ref, wc_ref, wu_ref, gb_ref, z_ref, h_ref):
    @pl.when(pl.program_id(2) == 0)
    def _():
        h_ref[...] = _modulate(x_ref[0], g_ref[...], sh_ref[0], sc_ref[0]).astype(BF16)

    h = h_ref[...]
    gb_ref[0] = jnp.dot(h, wb_ref[...], preferred_element_type=F32).astype(BF16)
    gc = jnp.dot(h, wc_ref[...], preferred_element_type=F32)
    u = jnp.dot(h, wu_ref[...], preferred_element_type=F32)
    z_ref[0] = (gc * u).astype(BF16)


def _conv_in(x, g, mod3, chunk0, w_in):
    b, l, d = x.shape
    tm, tn = 512, 1024
    nn = d // tn
    row = lambda bi, i, j: (bi, i, 0)
    col = lambda bi, i, j: (bi, i, j)
    return pl.pallas_call(
        _conv_in_kernel,
        out_shape=(jax.ShapeDtypeStruct((b, l, d), BF16), jax.ShapeDtypeStruct((b, l, d), BF16)),
        grid=(b, l // tm, nn),
        in_specs=[
            pl.BlockSpec((1, tm, d), row),
            pl.BlockSpec((1, d), lambda bi, i, j: (0, 0)),
            pl.BlockSpec((1, 1, d), lambda bi, i, j: (bi, 0, chunk0)),
            pl.BlockSpec((1, 1, d), lambda bi, i, j: (bi, 0, chunk0 + 1)),
            pl.BlockSpec((d, tn), lambda bi, i, j: (0, j)),
            pl.BlockSpec((d, tn), lambda bi, i, j: (0, nn + j)),
            pl.BlockSpec((d, tn), lambda bi, i, j: (0, 2 * nn + j)),
        ],
        out_specs=(pl.BlockSpec((1, tm, tn), col), pl.BlockSpec((1, tm, tn), col)),
        scratch_shapes=[pltpu.VMEM((tm, d), BF16)],
        compiler_params=_cparams("arbitrary", "arbitrary", "arbitrary"),
        name="conv_in",
    )(x, g, mod3, mod3, w_in, w_in, w_in)


def _conv_out_kernel(z_ref, zp_ref, zn_ref, gb_ref, cw_ref, w_ref, x_ref, g1_ref, out_ref):
    i = pl.program_id(1)
    last = pl.num_programs(1) - 1
    z = z_ref[0].astype(F32)
    tm = z.shape[0]
    prev_row = jnp.where(i > 0, zp_ref[0, SUBLANES - 1:SUBLANES, :].astype(F32), 0.0)
    next_row = jnp.where(i < last, zn_ref[0, 0:1, :].astype(F32), 0.0)
    ridx = lax.broadcasted_iota(jnp.int32, z.shape, 0)
    z_prev = jnp.where(ridx == 0, prev_row, pltpu.roll(z, 1, 0))
    z_next = jnp.where(ridx == tm - 1, next_row, pltpu.roll(z, tm - 1, 0))
    conv = cw_ref[0:1, :] * z_prev + cw_ref[1:2, :] * z + cw_ref[2:3, :] * z_next
    y = (gb_ref[0].astype(F32) * conv).astype(BF16)
    mix = jnp.dot(y, w_ref[...], preferred_element_type=F32)
    out_ref[0] = x_ref[0] + g1_ref[0] * mix


def _conv_out(z, gb, conv_w, w_out, x, mod3, gate_chunk):
    b, l, d = x.shape
    tm = 256
    hb = tm // SUBLANES
    n_halo = l // SUBLANES
    row = lambda bi, i: (bi, i, 0)
    return pl.pallas_call(
        _conv_out_kernel,
        out_shape=jax.ShapeDtypeStruct((b, l, d), F32),
        grid=(b, l // tm),
        in_specs=[
            pl.BlockSpec((1, tm, d), row),
            pl.BlockSpec((1, SUBLANES, d), lambda bi, i: (bi, jnp.maximum(i * hb - 1, 0), 0)),
            pl.BlockSpec((1, SUBLANES, d),
                         lambda bi, i: (bi, jnp.minimum((i + 1) * hb, n_halo - 1), 0)),
            pl.BlockSpec((1, tm, d), row),
            pl.BlockSpec(conv_w.shape, lambda bi, i: (0, 0)),
            pl.BlockSpec(w_out.shape, lambda bi, i: (0, 0)),
            pl.BlockSpec((1, tm, d), row),
            pl.BlockSpec((1, 1, d), lambda bi, i: (bi, 0, gate_chunk)),
        ],
        out_specs=pl.BlockSpec((1, tm, d), row),
        compiler_params=_cparams("arbitrary", "arbitrary"),
        name="conv_out",
    )(z, z, z, gb, conv_w, w_out, x, mod3)


def _pack_rows(h, hp_ref):
    tm, d = h.shape
    bits = pltpu.bitcast(h.astype(BF16).astype(F32), jnp.uint32)
    words = (bits[:, :d // 2] >> 16) | (bits[:, d // 2:] & jnp.uint32(0xFFFF0000))
    for s in range(ROW_WORDS):
        hp_ref[pl.ds(s, tm, stride=ROW_WORDS), :] = words[:, s * LANES:(s + 1) * LANES]


def _unpack_rows(buf_ref, n_rows):
    lows, highs = [], []
    for s in range(ROW_WORDS):
        w = buf_ref[pl.ds(s, n_rows, stride=ROW_WORDS), :]
        lows.append(pltpu.bitcast(w << 16, F32).astype(BF16))
        highs.append(pltpu.bitcast(w & jnp.uint32(0xFFFF0000), F32).astype(BF16))
    return jnp.concatenate(lows + highs, axis=-1)


def _router_kernel(x_ref, g_ref, sh_ref, sc_ref, whi_ref, wlo_ref, br_ref, hp_ref, idx_ref,
                   gate_ref):
    h = _modulate(x_ref[0], g_ref[...], sh_ref[0], sc_ref[0])
    _pack_rows(h, hp_ref)
    h_hi = h.astype(BF16)
    h_lo = (h - h_hi.astype(F32)).astype(BF16)
    logits = (jnp.dot(h_hi, whi_ref[...], preferred_element_type=F32)
              + (jnp.dot(h_lo, whi_ref[...], preferred_element_type=F32)
                 + jnp.dot(h_hi, wlo_ref[...], preferred_element_type=F32))) + br_ref[...]
    lane = lax.broadcasted_iota(jnp.int32, logits.shape, 1).astype(F32)
    work = jnp.where(lane < N_EXPERTS, logits, -jnp.inf)
    vals, idxs = [], []
    for _ in range(TOP_K):
        mx = jnp.max(work, axis=-1, keepdims=True)
        ix = jnp.min(jnp.where(work == mx, lane, float(LANES)), axis=-1, keepdims=True)
        vals.append(mx)
        idxs.append(ix.astype(jnp.int32))
        work = jnp.where(lane == ix, -jnp.inf, work)
    ex = [jnp.exp(v - vals[0]) for v in vals]
    den = ex[0] + ex[1] + ex[2] + ex[3]
    for k in range(TOP_K):
        idx_ref[:, k:k + 1] = idxs[k]
        gate_ref[:, k:k + 1] = ex[k] / den


def _router(x, g, mod3, chunk0, w_r, b_r):
    b, l, d = x.shape
    w_hi = w_r.astype(BF16)
    w_lo = (w_r - w_hi.astype(F32)).astype(BF16)
    tm = 512
    nt = l // tm
    t = b * l
    return pl.pallas_call(
        _router_kernel,
        out_shape=(jax.ShapeDtypeStruct((t * ROW_WORDS, LANES), jnp.uint32),
                   jax.ShapeDtypeStruct((t, TOP_K), jnp.int32),
                   jax.ShapeDtypeStruct((t, TOP_K), F32)),
        grid=(b, nt),
        in_specs=[
            pl.BlockSpec((1, tm, d), lambda bi, i: (bi, i, 0)),
            pl.BlockSpec((1, d), lambda bi, i: (0, 0)),
            pl.BlockSpec((1, 1, d), lambda bi, i: (bi, 0, chunk0)),
            pl.BlockSpec((1, 1, d), lambda bi, i: (bi, 0, chunk0 + 1)),
            pl.BlockSpec((d, LANES), lambda bi, i: (0, 0)),
            pl.BlockSpec((d, LANES), lambda bi, i: (0, 0)),
            pl.BlockSpec((1, LANES), lambda bi, i: (0, 0)),
        ],
        out_specs=(pl.BlockSpec((tm * ROW_WORDS, LANES), lambda bi, i: (bi * nt + i, 0)),
                   pl.BlockSpec((tm, TOP_K), lambda bi, i: (bi * nt + i, 0)),
                   pl.BlockSpec((tm, TOP_K), lambda bi, i: (bi * nt + i, 0))),
        compiler_params=_cparams("arbitrary", "arbitrary"),
        name="router",
    )(x, g, mod3, mod3, w_hi, w_lo, b_r)


def _issue_row_copies(n_rows, make_copy):
    def issue(it, carry):
        for u in range(DMA_UNROLL):
            make_copy(it * DMA_UNROLL + u).start(priority=u % 2)
        return carry

    lax.fori_loop(0, n_rows // DMA_UNROLL, issue, 0)


def _gather_kernel(nu_ref, tok_ref, tokn_ref, hp_ref, xs_ref, buf_ref, sem):
    i = pl.program_id(0)
    n_used = nu_ref[0]
    slot = lax.rem(i, 2)

    def issue_from(t_ref, dst_slot):
        def make_copy(r):
            src = pl.multiple_of(t_ref[0, 0, r] * ROW_WORDS, ROW_WORDS)
            return pltpu.make_async_copy(hp_ref.at[pl.ds(src, ROW_WORDS)],
                                         buf_ref.at[dst_slot, pl.ds(r * ROW_WORDS, ROW_WORDS)],
                                         sem.at[dst_slot])
        _issue_row_copies(GATHER_ROWS, make_copy)

    @pl.when(jnp.logical_and(i == 0, n_used > 0))
    def _():
        issue_from(tok_ref, 0)

    @pl.when(i + 1 < n_used)
    def _():
        issue_from(tokn_ref, 1 - slot)

    @pl.when(i < n_used)
    def _():
        pltpu.make_async_copy(hp_ref.at[pl.ds(0, GATHER_ROWS * ROW_WORDS)], buf_ref.at[slot],
                              sem.at[slot]).wait()
        xs_ref[...] = _unpack_rows(buf_ref.at[slot], GATHER_ROWS)

    @pl.when(i >= n_used)
    def _():
        xs_ref[...] = jnp.zeros_like(xs_ref)


def _gather_rows(n_used, slot_tok, hp, d):
    n_slots = slot_tok.shape[0]
    n_steps = n_slots // GATHER_ROWS
    tok3 = slot_tok.reshape(n_steps, 1, GATHER_ROWS)
    return pl.pallas_call(
        _gather_kernel,
        out_shape=jax.ShapeDtypeStruct((n_slots, d), BF16),
        grid_spec=pltpu.PrefetchScalarGridSpec(
            num_scalar_prefetch=1,
            grid=(n_steps,),
            in_specs=[
                pl.BlockSpec((1, 1, GATHER_ROWS), lambda i, nu: (i, 0, 0), memory_space=pltpu.SMEM),
                pl.BlockSpec((1, 1, GATHER_ROWS),
                             lambda i, nu: (jnp.minimum(i + 1, n_steps - 1), 0, 0),
                             memory_space=pltpu.SMEM),
                pl.BlockSpec(memory_space=pl.ANY),
            ],
            out_specs=pl.BlockSpec((GATHER_ROWS, d), lambda i, nu: (i, 0)),
            scratch_shapes=[pltpu.VMEM((2, GATHER_ROWS * ROW_WORDS, LANES), jnp.uint32),
                            pltpu.SemaphoreType.DMA((2,))],
        ),
        compiler_params=_cparams("arbitrary"),
        name="gather_rows",
    )(n_used, tok3, tok3, hp)


def _block_state(be_ref):
    bi = pl.program_id(1)
    used = bi < be_ref[pl.num_programs(1)]
    prev = be_ref[jnp.maximum(bi - 1, 0)]
    return used, jnp.logical_or(bi == 0, be_ref[bi] != prev)


def _gmm_gu_kernel(be_ref, x_ref, wg_ref, wl_ref, bg_ref, bl_ref, act_ref, wg_s, wl_s):
    used, changed = _block_state(be_ref)

    @pl.when(jnp.logical_and(used, changed))
    def _():
        wg_s[...] = wg_ref[...].astype(BF16)
        wl_s[...] = wl_ref[...].astype(BF16)

    @pl.when(used)
    def _():
        xb = x_ref[...]
        glu = jnp.dot(xb, wg_s[...], preferred_element_type=F32) + bg_ref[...]
        lin = jnp.dot(xb, wl_s[...], preferred_element_type=F32) + bl_ref[...]
        glu = jnp.minimum(glu, SWIGLU_LIMIT)
        lin = jnp.clip(lin, -SWIGLU_LIMIT, SWIGLU_LIMIT)
        sig = 1.0 / (1.0 + jnp.exp(-SWIGLU_ALPHA * glu))
        act_ref[...] = (glu * sig * (lin + 1.0)).astype(BF16)

    @pl.when(jnp.logical_not(used))
    def _():
        act_ref[...] = jnp.zeros_like(act_ref)


def _gmm_gu(block_e, xs, w_gu, b_gu4, layer):
    n_slots, d = xs.shape
    f = w_gu.shape[3] // 2
    tm, tn = EXPERT_TM, 1024
    nn = f // tn
    return pl.pallas_call(
        _gmm_gu_kernel,
        out_shape=jax.ShapeDtypeStruct((n_slots, f), BF16),
        grid_spec=pltpu.PrefetchScalarGridSpec(
            num_scalar_prefetch=1,
            grid=(nn, n_slots // tm),
            in_specs=[
                pl.BlockSpec((tm, d), lambda j, i, be: (i, 0)),
                pl.BlockSpec((None, None, d, tn), lambda j, i, be: (layer, be[i], 0, j)),
                pl.BlockSpec((None, None, d, tn), lambda j, i, be: (layer, be[i], 0, nn + j)),
                pl.BlockSpec((None, None, 1, tn), lambda j, i, be: (layer, be[i], 0, j)),
                pl.BlockSpec((None, None, 1, tn), lambda j, i, be: (layer, be[i], 0, nn + j)),
            ],
            out_specs=pl.BlockSpec((tm, tn), lambda j, i, be: (i, j)),
            scratch_shapes=[pltpu.VMEM((d, tn), BF16), pltpu.VMEM((d, tn), BF16)],
        ),
        compiler_params=_cparams("arbitrary", "arbitrary"),
        name="gmm_gate_up",
    )(block_e, xs, w_gu, w_gu, b_gu4, b_gu4)


DOWN_TN = SUBLANES * LANES


def _gmm_down_kernel(be_ref, a_ref, w_ref, b_ref, y_ref, w_s):
    used, changed = _block_state(be_ref)

    @pl.when(jnp.logical_and(used, changed))
    def _():
        w_s[...] = w_ref[...].astype(BF16)

    @pl.when(used)
    def _():
        y = jnp.dot(a_ref[...], w_s[...], preferred_element_type=F32) + b_ref[...]
        tm = y.shape[0]
        y2_ref = y_ref.reshape(tm * SUBLANES, LANES)
        for cc in range(SUBLANES):
            y2_ref[pl.ds(cc, tm, stride=SUBLANES), :] = y[:, cc * LANES:(cc + 1) * LANES]

    @pl.when(jnp.logical_not(used))
    def _():
        y_ref[...] = jnp.zeros_like(y_ref)


def _gmm_down(block_e, act, w_down, b_down4, layer):
    n_slots, f = act.shape
    d = w_down.shape[3]
    tm, tn = EXPERT_TM, DOWN_TN
    return pl.pallas_call(
        _gmm_down_kernel,
        out_shape=jax.ShapeDtypeStruct((n_slots, d // LANES, LANES), F32),
        grid_spec=pltpu.PrefetchScalarGridSpec(
            num_scalar_prefetch=1,
            grid=(d // tn, n_slots // tm),
            in_specs=[
                pl.BlockSpec((tm, f), lambda j, i, be: (i, 0)),
                pl.BlockSpec((None, None, f, tn), lambda j, i, be: (layer, be[i], 0, j)),
                pl.BlockSpec((None, None, 1, tn), lambda j, i, be: (layer, be[i], 0, j)),
            ],
            out_specs=pl.BlockSpec((tm, SUBLANES, LANES), lambda j, i, be: (i, j, 0)),
            scratch_shapes=[pltpu.VMEM((f, tn), BF16)],
        ),
        compiler_params=_cparams("arbitrary", "arbitrary"),
        name="gmm_down",
    )(block_e, act, w_down, b_down4)


COMBINE_ROWS = COMBINE_TOK * TOP_K


def _combine_kernel(pos_ref, posn_ref, ys_ref, gate_ref, x_ref, g2_ref, gf_ref, out_ref,
                    buf_ref, sem, *, final_norm):
    i = pl.program_id(0)
    n = pl.num_programs(0)
    slot = lax.rem(i, 2)
    n_chunks = x_ref.shape[1] // LANES

    def issue_from(p_ref, dst_slot):
        def make_copy(r):
            src = pl.multiple_of(p_ref[0, 0, r] * n_chunks, n_chunks)
            return pltpu.make_async_copy(ys_ref.at[pl.ds(src, n_chunks)],
                                         buf_ref.at[dst_slot, pl.ds(r * n_chunks, n_chunks)],
                                         sem.at[dst_slot])
        _issue_row_copies(COMBINE_ROWS, make_copy)

    @pl.when(i == 0)
    def _():
        issue_from(pos_ref, 0)

    @pl.when(i + 1 < n)
    def _():
        issue_from(posn_ref, 1 - slot)

    pltpu.make_async_copy(ys_ref.at[pl.ds(0, COMBINE_ROWS * n_chunks)], buf_ref.at[slot],
                          sem.at[slot]).wait()

    rows_ref = buf_ref.at[slot]
    gates = [jnp.broadcast_to(gate_ref[:, k:k + 1], (COMBINE_TOK, LANES)) for k in range(TOP_K)]
    ssq = jnp.zeros((COMBINE_TOK, LANES), F32)
    for c in range(n_chunks):
        cols = slice(c * LANES, (c + 1) * LANES)
        y = None
        for k in range(TOP_K):
            rows = rows_ref[pl.ds(k * COMBINE_TOK * n_chunks + c, COMBINE_TOK, stride=n_chunks), :]
            y = gates[k] * rows if y is None else y + gates[k] * rows
        o = x_ref[:, cols] + g2_ref[0, :, cols] * y
        out_ref[:, cols] = o
        ssq = ssq + o * o
    if final_norm:
        d = n_chunks * LANES
        inv = lax.rsqrt(jnp.sum(ssq, axis=-1, keepdims=True) / d + RMS_EPS)
        out_ref[...] = out_ref[...] * inv * gf_ref[...]


def _combine(pos, ys2, gate, x2, mod3, gate_chunk, tokens_per_batch, final_g, final_norm):
    t, d = x2.shape
    n_chunks = d // LANES
    n_steps = t // COMBINE_TOK
    steps_per_batch = tokens_per_batch // COMBINE_TOK
    pos3 = pos.reshape(n_steps, COMBINE_TOK, TOP_K).transpose(0, 2, 1).reshape(n_steps, 1, COMBINE_ROWS)
    return pl.pallas_call(
        functools.partial(_combine_kernel, final_norm=final_norm),
        out_shape=jax.ShapeDtypeStruct((t, d), F32),
        grid=(n_steps,),
        in_specs=[
            pl.BlockSpec((1, 1, COMBINE_ROWS), lambda i: (i, 0, 0), memory_space=pltpu.SMEM),
            pl.BlockSpec((1, 1, COMBINE_ROWS), lambda i: (jnp.minimum(i + 1, n_steps - 1), 0, 0),
                         memory_space=pltpu.SMEM),
            pl.BlockSpec(memory_space=pl.ANY),
            pl.BlockSpec((COMBINE_TOK, TOP_K), lambda i: (i, 0)),
            pl.BlockSpec((COMBINE_TOK, d), lambda i: (i, 0)),
            pl.BlockSpec((1, 1, d), lambda i: (i // steps_per_batch, 0, gate_chunk)),
            pl.BlockSpec((1, d), lambda i: (0, 0)),
        ],
        out_specs=pl.BlockSpec((COMBINE_TOK, d), lambda i: (i, 0)),
        scratch_shapes=[pltpu.VMEM((2, COMBINE_ROWS * n_chunks, LANES), F32),
                        pltpu.SemaphoreType.DMA((2,))],
        compiler_params=_cparams("arbitrary"),
        name="combine",
    )(pos3, pos3, ys2, gate, x2, mod3, final_g)


SCATTER_CHUNK = 8192
TOP_K_SHIFT = 2
assert 1 << TOP_K_SHIFT == TOP_K and DMA_UNROLL % TOP_K == 0


def _slot_tok_kernel(pos_ref, zeros_ref, out_ref):
    i = pl.program_id(0)

    @pl.when(i == 0)
    def _():
        pltpu.sync_copy(zeros_ref, out_ref)

    base = i * SCATTER_CHUNK

    def scatter(it, carry):
        a0 = it * DMA_UNROLL
        tok0 = lax.shift_right_logical(base + a0, TOP_K_SHIFT)
        for u in range(DMA_UNROLL):
            out_ref[pos_ref[0, 0, a0 + u]] = tok0 + (u >> TOP_K_SHIFT)
        return carry

    lax.fori_loop(0, SCATTER_CHUNK // DMA_UNROLL, scatter, 0)


def _slot_tokens(pos, n_slots):
    n_steps = pos.shape[0] // SCATTER_CHUNK
    return pl.pallas_call(
        _slot_tok_kernel,
        out_shape=jax.ShapeDtypeStruct((n_slots,), jnp.int32),
        grid=(n_steps,),
        in_specs=[pl.BlockSpec((1, 1, SCATTER_CHUNK), lambda i: (i, 0, 0),
                               memory_space=pltpu.SMEM),
                  pl.BlockSpec(memory_space=pl.ANY)],
        out_specs=pl.BlockSpec(memory_space=pltpu.SMEM),
        compiler_params=_cparams("arbitrary"),
        name="slot_tokens",
    )(pos.reshape(n_steps, 1, SCATTER_CHUNK), jnp.zeros((n_slots,), jnp.int32))


def _slot_plan(top_idx):
    t = top_idx.shape[0]
    n_assign = t * TOP_K
    flat_e = top_idx.reshape(-1)
    onehot = (flat_e[:, None] == jnp.arange(N_EXPERTS, dtype=jnp.int32)[None, :]).astype(jnp.int32)
    csum = jnp.cumsum(onehot, axis=0)
    counts = csum[-1]
    rank = jnp.sum(csum * onehot, axis=1) - 1
    padded = (counts + EXPERT_TM - 1) // EXPERT_TM * EXPERT_TM
    padded_end = jnp.cumsum(padded)
    padded_start = padded_end - padded
    pos = (padded_start[flat_e] + rank).astype(jnp.int32)
    n_blocks = -(-(n_assign + N_EXPERTS * (EXPERT_TM - 1)) // EXPERT_TM)
    n_blocks = -(-n_blocks * EXPERT_TM // GATHER_ROWS) * GATHER_ROWS // EXPERT_TM
    n_slots = n_blocks * EXPERT_TM
    slot_tok = _slot_tokens(pos, n_slots)
    block_start = jnp.arange(n_blocks, dtype=jnp.int32) * EXPERT_TM
    block_e = jnp.minimum(jnp.sum(block_start[:, None] >= padded_end[None, :], axis=1),
                          N_EXPERTS - 1).astype(jnp.int32)
    n_used = (padded_end[-1:] // EXPERT_TM).astype(jnp.int32)
    return pos, slot_tok, jnp.concatenate([block_e, n_used]), n_used


def _expert_ffn_residual(x, mod3, layer, norm_g, w_r, b_r, w_gu, b_gu4, w_down, b_down4,
                         final_g, final_norm):
    b, l, d = x.shape
    assert d == 2 * ROW_WORDS * LANES
    hp, top_idx, gate = _router(x, norm_g, mod3, 3, w_r, b_r)
    pos, slot_tok, block_e, n_used = _slot_plan(top_idx)
    xs = _gather_rows(n_used, slot_tok, hp, d)
    act = _gmm_gu(block_e, xs, w_gu, b_gu4, layer)
    ys3 = _gmm_down(block_e, act, w_down, b_down4, layer)
    ys2 = ys3.reshape(ys3.shape[0] * ys3.shape[1], LANES)
    out = _combine(pos, ys2, gate, x.reshape(b * l, d), mod3, 5, l, final_g, final_norm)
    return out.reshape(b, l, d)


def _rope_tables(length):
    rows = length // GRID_W
    row = jnp.repeat(jnp.arange(rows), GRID_W).astype(F32)
    col = jnp.tile(jnp.arange(GRID_W), rows).astype(F32)
    inv = 1.0 / (ROPE_THETA ** (jnp.arange(0, ROPE_AXIS, 2, dtype=F32) / ROPE_AXIS))
    ang_r = row[:, None] * inv[None, :]
    ang_c = col[:, None] * inv[None, :]
    cos = jnp.concatenate([jnp.cos(ang_r)] * 2 + [jnp.cos(ang_c)] * 2, axis=-1)
    sin = jnp.concatenate([jnp.sin(ang_r)] * 2 + [jnp.sin(ang_c)] * 2, axis=-1)
    return cos, sin


def _rotate_half_cols(w):
    a, b_, c_, d_ = jnp.split(w, 4, axis=-1)
    return jnp.concatenate([-b_, a, -d_, c_], axis=-1)


def kernel(x, c, ctx, c_ctx, ada_w, ada_b, norm_mix_g, norm_ffn_g, mla_w_in, mla_q_norm_g,
           mla_kv_norm_g, mla_w_q_up, mla_w_kv_up, mla_w_out, conv_w_in, conv_w, conv_w_out,
           router_w, router_b, expert_w_gu, expert_b_gu, expert_w_down, expert_b_down,
           final_norm_g):
    b, l, d = x.shape
    depth = ada_w.shape[0]
    lc = ctx.shape[1]
    assert depth == 2 and b + 1 <= ADA_ROWS

    cv = jnp.concatenate([c, c_ctx[None, :], jnp.zeros((ADA_ROWS - b - 1, d), F32)], axis=0)
    ctx_row = b
    ada_b3 = ada_b.reshape(depth, 1, N_ADA * d)
    cos, sin = _rope_tables(l)
    b_gu4 = expert_b_gu.reshape(depth, N_EXPERTS, 1, -1)
    b_down4 = expert_b_down.reshape(depth, N_EXPERTS, 1, d)
    w_r = jnp.pad(router_w, ((0, 0), (0, 0), (0, LANES - N_EXPERTS)))
    b_r = jnp.pad(router_b, ((0, 0), (0, LANES - N_EXPERTS)))[:, None, :]
    final_g = final_norm_g[None, :]

    mod3 = _ada(cv, ada_w, ada_b3, 0).reshape(ADA_ROWS, 1, N_ADA * d)
    w_in = mla_w_in[0]
    w_kr = w_in[:, Q_LORA + KV_LORA:]
    w_all = jnp.concatenate([w_in, _rotate_half_cols(w_kr)], axis=1).astype(BF16)
    w_ctx = w_in[:, Q_LORA:].astype(BF16)
    gq = mla_q_norm_g[0][None, :]
    gkv = mla_kv_norm_g[0][None, :]
    g_mix = norm_mix_g[0][None, :]
    wq = mla_w_q_up[0].reshape(Q_LORA, N_HEADS, QK_DIM)
    wq_rope = wq[..., QK_NOPE:]
    wq_all = jnp.concatenate([wq, _rotate_half_cols(wq_rope)], axis=-1)
    wq_all = wq_all.reshape(Q_LORA, N_HEADS * Q_HEAD_COLS).astype(BF16)
    wkv = mla_w_kv_up[0].astype(BF16)
    w_o = mla_w_out[0].astype(BF16)

    cq, ckv, kr = _mla_in(x, g_mix, mod3, 0, w_all, gq, gkv, cos, sin)
    ckv_c, kr_c = _mla_in_ctx(ctx, g_mix, mod3, ctx_row, w_ctx, gkv)
    ckv_all = jnp.concatenate([ckv_c, ckv], axis=1)
    kr_all = jnp.concatenate([kr_c, kr], axis=1)
    q = _q_up(cq, wq_all, cos, sin)
    k, v = _kv_up(ckv_all, kr_all, wkv, 768)
    o = _attention(q, k, v, 1024, 768)
    x = _proj_res(o, w_o, x, mod3, 2)
    x = _expert_ffn_residual(x, mod3, 0, norm_ffn_g[0][None, :], w_r[0], b_r[0], expert_w_gu,
                             b_gu4, expert_w_down, b_down4, final_g, False)

    mod3 = _ada(cv, ada_w, ada_b3, 1).reshape(ADA_ROWS, 1, N_ADA * d)
    gb, z = _conv_in(x, norm_mix_g[1][None, :], mod3, 0, conv_w_in[0].astype(BF16))
    x = _conv_out(z, gb, conv_w[0], conv_w_out[0].astype(BF16), x, mod3, 2)
    x = _expert_ffn_residual(x, mod3, 1, norm_ffn_g[1][None, :], w_r[1], b_r[1], expert_w_gu,
                             b_gu4, expert_w_down, b_down4, final_g, True)
    return x
```

```python
import functools

import jax
import jax.numpy as jnp
import numpy as np
from jax import lax
from jax.experimental import pallas as pl
from jax.experimental.pallas import tpu as pltpu

GRID_W = 64
RMS_EPS = 1e-6
N_ADA = 6
N_HEADS = 16
Q_LORA = 512
KV_LORA = 512
QK_NOPE = 128
QK_ROPE = 64
V_DIM = 128
ROPE_AXIS = QK_ROPE // 2
ROPE_THETA = 10000.0
QK_DIM = QK_NOPE + QK_ROPE
ATTN_SCALE = QK_DIM ** -0.5
LOG2_E = 1.4426950408889634
Q_SCALE = ATTN_SCALE * LOG2_E
N_EXPERTS = 32
TOP_K = 4
SWIGLU_LIMIT = 7.0
SWIGLU_ALPHA = 1.702

LANES = 128
SUBLANES = 8
VMEM_LIMIT = 56 * 1024 * 1024

ADA_ROWS = 8
EXPERT_TM = 512
GATHER_ROWS = EXPERT_TM
COMBINE_TOK = 128
ROW_WORDS = SUBLANES
DMA_UNROLL = 8

BF16 = jnp.bfloat16
F32 = jnp.float32


def _cparams(*sem):
    return pltpu.CompilerParams(dimension_semantics=sem, vmem_limit_bytes=VMEM_LIMIT)


def _rms(x):
    return x * lax.rsqrt(jnp.mean(x * x, axis=-1, keepdims=True) + RMS_EPS)


def _modulate(x, g, shift, scale):
    return _rms(x) * g * (1.0 + scale) + shift


def _ada_kernel(cv_ref, w_ref, b_ref, o_ref):
    cv = cv_ref[...]
    s = cv * (1.0 / (1.0 + jnp.exp(-cv)))
    o_ref[...] = jnp.dot(s, w_ref[...], preferred_element_type=F32,
                         precision=lax.Precision.HIGHEST) + b_ref[...]


def _ada(cv, ada_w, ada_b3, layer):
    d = cv.shape[1]
    n = ada_w.shape[2]
    tn = 1024
    return pl.pallas_call(
        _ada_kernel,
        out_shape=jax.ShapeDtypeStruct((ADA_ROWS, n), F32),
        grid=(n // tn,),
        in_specs=[
            pl.BlockSpec((ADA_ROWS, d), lambda j: (0, 0)),
            pl.BlockSpec((None, d, tn), lambda j: (layer, 0, j)),
            pl.BlockSpec((None, 1, tn), lambda j: (layer, 0, j)),
        ],
        out_specs=pl.BlockSpec((ADA_ROWS, tn), lambda j: (0, j)),
        compiler_params=_cparams("arbitrary"),
        name="ada",
    )(cv, ada_w, ada_b3)


def _mla_in_kernel(x_ref, g_ref, sh_ref, sc_ref, w_ref, gq_ref, gkv_ref, cos_ref, sin_ref,
                   cq_ref, ckv_ref, kr_ref):
    h = _modulate(x_ref[0], g_ref[...], sh_ref[0], sc_ref[0]).astype(BF16)
    p = jnp.dot(h, w_ref[...], preferred_element_type=F32)
    cq_ref[0] = (_rms(p[:, :Q_LORA]) * gq_ref[...]).astype(BF16)
    ckv_ref[0] = (_rms(p[:, Q_LORA:Q_LORA + KV_LORA]) * gkv_ref[...]).astype(BF16)
    o = Q_LORA + KV_LORA
    kr = p[:, o:o + QK_ROPE]
    kr_rot = p[:, o + QK_ROPE:o + 2 * QK_ROPE]
    kr_ref[0] = (kr * cos_ref[...] + kr_rot * sin_ref[...]).astype(BF16)


def _mla_in_ctx_kernel(x_ref, g_ref, sh_ref, sc_ref, w_ref, gkv_ref, ckv_ref, kr_ref):
    h = _modulate(x_ref[0], g_ref[...], sh_ref[0], sc_ref[0]).astype(BF16)
    p = jnp.dot(h, w_ref[...], preferred_element_type=F32)
    ckv_ref[0] = (_rms(p[:, :KV_LORA]) * gkv_ref[...]).astype(BF16)
    kr_ref[0] = p[:, KV_LORA:KV_LORA + QK_ROPE].astype(BF16)


def _mla_in(x, g, mod3, layer_chunk0, w_all, gq, gkv, cos, sin):
    b, l, d = x.shape
    tm = 512
    nw = w_all.shape[1]
    row = lambda bi, i: (bi, i, 0)
    return pl.pallas_call(
        _mla_in_kernel,
        out_shape=(jax.ShapeDtypeStruct((b, l, Q_LORA), BF16),
                   jax.ShapeDtypeStruct((b, l, KV_LORA), BF16),
                   jax.ShapeDtypeStruct((b, l, QK_ROPE), BF16)),
        grid=(b, l // tm),
        in_specs=[
            pl.BlockSpec((1, tm, d), row),
            pl.BlockSpec((1, d), lambda bi, i: (0, 0)),
            pl.BlockSpec((1, 1, d), lambda bi, i: (bi, 0, layer_chunk0)),
            pl.BlockSpec((1, 1, d), lambda bi, i: (bi, 0, layer_chunk0 + 1)),
            pl.BlockSpec((d, nw), lambda bi, i: (0, 0)),
            pl.BlockSpec((1, Q_LORA), lambda bi, i: (0, 0)),
            pl.BlockSpec((1, KV_LORA), lambda bi, i: (0, 0)),
            pl.BlockSpec((tm, QK_ROPE), lambda bi, i: (i, 0)),
            pl.BlockSpec((tm, QK_ROPE), lambda bi, i: (i, 0)),
        ],
        out_specs=(pl.BlockSpec((1, tm, Q_LORA), row),
                   pl.BlockSpec((1, tm, KV_LORA), row),
                   pl.BlockSpec((1, tm, QK_ROPE), row)),
        compiler_params=_cparams("arbitrary", "arbitrary"),
        name="mla_in",
    )(x, g, mod3, mod3, w_all, gq, gkv, cos, sin)


def _mla_in_ctx(ctx, g, mod3, ctx_row, w_kv, gkv):
    b, l, d = ctx.shape
    tm = l
    nw = w_kv.shape[1]
    row = lambda bi, i: (bi, i, 0)
    return pl.pallas_call(
        _mla_in_ctx_kernel,
        out_shape=(jax.ShapeDtypeStruct((b, l, KV_LORA), BF16),
                   jax.ShapeDtypeStruct((b, l, QK_ROPE), BF16)),
        grid=(b, l // tm),
        in_specs=[
            pl.BlockSpec((1, tm, d), row),
            pl.BlockSpec((1, d), lambda bi, i: (0, 0)),
            pl.BlockSpec((1, 1, d), lambda bi, i: (ctx_row, 0, 0)),
            pl.BlockSpec((1, 1, d), lambda bi, i: (ctx_row, 0, 1)),
            pl.BlockSpec((d, nw), lambda bi, i: (0, 0)),
            pl.BlockSpec((1, KV_LORA), lambda bi, i: (0, 0)),
        ],
        out_specs=(pl.BlockSpec((1, tm, KV_LORA), row),
                   pl.BlockSpec((1, tm, QK_ROPE), row)),
        compiler_params=_cparams("arbitrary", "arbitrary"),
        name="mla_in_ctx",
    )(ctx, g, mod3, mod3, w_kv, gkv)


Q_HEAD_COLS = QK_NOPE + 2 * QK_ROPE
KV_HEAD_COLS = QK_NOPE + V_DIM


def _q_up_kernel(cq_ref, w_ref, cos_ref, sin_ref, q_ref):
    cq = cq_ref[0]
    cos = cos_ref[...]
    sin = sin_ref[...]
    for h in range(N_HEADS):
        p = jnp.dot(cq, w_ref[:, h * Q_HEAD_COLS:(h + 1) * Q_HEAD_COLS],
                    preferred_element_type=F32)
        rope = p[:, QK_NOPE:QK_NOPE + QK_ROPE] * cos + p[:, QK_NOPE + QK_ROPE:] * sin
        q_ref[0, h, :, :QK_NOPE] = (p[:, :QK_NOPE] * Q_SCALE).astype(BF16)
        q_ref[0, h, :, QK_NOPE:] = (rope * Q_SCALE).astype(BF16)


def _q_up(cq, w_q, cos, sin):
    b, l, _ = cq.shape
    tm = 512
    return pl.pallas_call(
        _q_up_kernel,
        out_shape=jax.ShapeDtypeStruct((b, N_HEADS, l, QK_DIM), BF16),
        grid=(b, l // tm),
        in_specs=[
            pl.BlockSpec((1, tm, Q_LORA), lambda bi, i: (bi, i, 0)),
            pl.BlockSpec(w_q.shape, lambda bi, i: (0, 0)),
            pl.BlockSpec((tm, QK_ROPE), lambda bi, i: (i, 0)),
            pl.BlockSpec((tm, QK_ROPE), lambda bi, i: (i, 0)),
        ],
        out_specs=pl.BlockSpec((1, N_HEADS, tm, QK_DIM), lambda bi, i: (bi, 0, i, 0)),
        compiler_params=_cparams("arbitrary", "arbitrary"),
        name="q_up",
    )(cq, w_q, cos, sin)


def _kv_up_kernel(ckv_ref, kr_ref, w_ref, k_ref, v_ref):
    ckv = ckv_ref[0]
    kr = kr_ref[0]
    for h in range(N_HEADS):
        p = jnp.dot(ckv, w_ref[:, h * KV_HEAD_COLS:(h + 1) * KV_HEAD_COLS],
                    preferred_element_type=F32)
        k_ref[0, h, :, :QK_NOPE] = p[:, :QK_NOPE].astype(BF16)
        k_ref[0, h, :, QK_NOPE:] = kr
        v_ref[0, h] = p[:, QK_NOPE:].astype(BF16)


def _kv_up(ckv, kr, w_kv, tm):
    b, l, _ = ckv.shape
    return pl.pallas_call(
        _kv_up_kernel,
        out_shape=(jax.ShapeDtypeStruct((b, N_HEADS, l, QK_DIM), BF16),
                   jax.ShapeDtypeStruct((b, N_HEADS, l, V_DIM), BF16)),
        grid=(b, l // tm),
        in_specs=[
            pl.BlockSpec((1, tm, KV_LORA), lambda bi, i: (bi, i, 0)),
            pl.BlockSpec((1, tm, QK_ROPE), lambda bi, i: (bi, i, 0)),
            pl.BlockSpec(w_kv.shape, lambda bi, i: (0, 0)),
        ],
        out_specs=(pl.BlockSpec((1, N_HEADS, tm, QK_DIM), lambda bi, i: (bi, 0, i, 0)),
                   pl.BlockSpec((1, N_HEADS, tm, V_DIM), lambda bi, i: (bi, 0, i, 0))),
        compiler_params=_cparams("arbitrary", "arbitrary"),
        name="kv_up",
    )(ckv, kr, w_kv)


def _attn_kernel(q_ref, k_ref, v_ref, o_ref, *, tk, n_chunks):
    q = q_ref[0, 0]
    tq = q.shape[0]
    n_slabs = tk // LANES
    m = jnp.full((tq, LANES), -jnp.inf, F32)
    l_loc = jnp.zeros((tq, LANES), F32)
    acc = jnp.zeros((tq, V_DIM), F32)
    for c in range(n_chunks):
        k = k_ref[0, 0, c * tk:(c + 1) * tk, :]
        s = lax.dot_general(q, k, (((1,), (1,)), ((), ())), preferred_element_type=F32)
        slabs = [s[:, j * LANES:(j + 1) * LANES] for j in range(n_slabs)]
        m_loc = slabs[0]
        for slab in slabs[1:]:
            m_loc = jnp.maximum(m_loc, slab)
        m_new = jnp.maximum(m, jnp.max(m_loc, axis=-1, keepdims=True))
        alpha = jnp.exp2(m - m_new)
        ps = [jnp.exp2(slab - m_new) for slab in slabs]
        p_sum = ps[0]
        for p in ps[1:]:
            p_sum = p_sum + p
        l_loc = alpha * l_loc + p_sum
        p_bf = jnp.concatenate([p.astype(BF16) for p in ps], axis=-1)
        acc = alpha * acc + jnp.dot(p_bf, v_ref[0, 0, c * tk:(c + 1) * tk, :],
                                    preferred_element_type=F32)
        m = m_new
    o_ref[0] = (acc / jnp.sum(l_loc, axis=-1, keepdims=True)).astype(BF16)


def _attention(q, k, v, tq, tk):
    b, h, l, _ = q.shape
    lk = k.shape[2]
    return pl.pallas_call(
        functools.partial(_attn_kernel, tk=tk, n_chunks=lk // tk),
        out_shape=jax.ShapeDtypeStruct((b, l, h * V_DIM), BF16),
        grid=(b, h, l // tq),
        in_specs=[
            pl.BlockSpec((1, 1, tq, QK_DIM), lambda bi, hi, i: (bi, hi, i, 0)),
            pl.BlockSpec((1, 1, lk, QK_DIM), lambda bi, hi, i: (bi, hi, 0, 0)),
            pl.BlockSpec((1, 1, lk, V_DIM), lambda bi, hi, i: (bi, hi, 0, 0)),
        ],
        out_specs=pl.BlockSpec((1, tq, V_DIM), lambda bi, hi, i: (bi, i, hi)),
        compiler_params=_cparams("arbitrary", "arbitrary", "arbitrary"),
        name="attention",
    )(q, k, v)


def _proj_res_kernel(o_ref, w_ref, x_ref, g1_ref, out_ref):
    mix = jnp.dot(o_ref[0], w_ref[...], preferred_element_type=F32)
    out_ref[0] = x_ref[0] + g1_ref[0] * mix


def _proj_res(o, w_out, x, mod3, gate_chunk):
    b, l, d = x.shape
    tm = 512
    row = lambda bi, i: (bi, i, 0)
    return pl.pallas_call(
        _proj_res_kernel,
        out_shape=jax.ShapeDtypeStruct((b, l, d), F32),
        grid=(b, l // tm),
        in_specs=[
            pl.BlockSpec((1, tm, o.shape[2]), row),
            pl.BlockSpec(w_out.shape, lambda bi, i: (0, 0)),
            pl.BlockSpec((1, tm, d), row),
            pl.BlockSpec((1, 1, d), lambda bi, i: (bi, 0, gate_chunk)),
        ],
        out_specs=pl.BlockSpec((1, tm, d), row),
        compiler_params=_cparams("arbitrary", "arbitrary"),
        name="proj_res",
    )(o, w_out, x, mod3)


def _conv_in_kernel(x_ref, g_ref, sh_ref, sc_ref, wb_ref, wc_ref, wu_ref, gb_ref, z_ref, h_ref):
    @pl.when(pl.program_id(2) == 0)
    def _():
        h_ref[...] = _modulate(x_ref[0], g_ref[...], sh_ref[0], sc_ref[0]).astype(BF16)

    h = h_ref[...]
    gb_ref[0] = jnp.dot(h, wb_ref[...], preferred_element_type=F32).astype(BF16)
    gc = jnp.dot(h, wc_ref[...], preferred_element_type=F32)
    u = jnp.dot(h, wu_ref[...], preferred_element_type=F32)
    z_ref[0] = (gc * u).astype(BF16)


def _conv_in(x, g, mod3, chunk0, w_in):
    b, l, d = x.shape
    tm, tn = 512, 1024
    nn = d // tn
    row = lambda bi, i, j: (bi, i, 0)
    col = lambda bi, i, j: (bi, i, j)
    return pl.pallas_call(
        _conv_in_kernel,
        out_shape=(jax.ShapeDtypeStruct((b, l, d), BF16), jax.ShapeDtypeStruct((b, l, d), BF16)),
        grid=(b, l // tm, nn),
        in_specs=[
            pl.BlockSpec((1, tm, d), row),
            pl.BlockSpec((1, d), lambda bi, i, j: (0, 0)),
            pl.BlockSpec((1, 1, d), lambda bi, i, j: (bi, 0, chunk0)),
            pl.BlockSpec((1, 1, d), lambda bi, i, j: (bi, 0, chunk0 + 1)),
            pl.BlockSpec((d, tn), lambda bi, i, j: (0, j)),
            pl.BlockSpec((d, tn), lambda bi, i, j: (0, nn + j)),
            pl.BlockSpec((d, tn), lambda bi, i, j: (0, 2 * nn + j)),
        ],
        out_specs=(pl.BlockSpec((1, tm, tn), col), pl.BlockSpec((1, tm, tn), col)),
        scratch_shapes=[pltpu.VMEM((tm, d), BF16)],
        compiler_params=_cparams("arbitrary", "arbitrary", "arbitrary"),
        name="conv_in",
    )(x, g, mod3, mod3, w_in, w_in, w_in)


def _conv_out_kernel(z_ref, zp_ref, zn_ref, gb_ref, cw_ref, w_ref, x_ref, g1_ref, out_ref):
    i = pl.program_id(1)
    last = pl.num_programs(1) - 1
    z = z_ref[0].astype(F32)
    tm = z.shape[0]
    prev_row = jnp.where(i > 0, zp_ref[0, SUBLANES - 1:SUBLANES, :].astype(F32), 0.0)
    next_row = jnp.where(i < last, zn_ref[0, 0:1, :].astype(F32), 0.0)
    ridx = lax.broadcasted_iota(jnp.int32, z.shape, 0)
    z_prev = jnp.where(ridx == 0, prev_row, pltpu.roll(z, 1, 0))
    z_next = jnp.where(ridx == tm - 1, next_row, pltpu.roll(z, tm - 1, 0))
    conv = cw_ref[0:1, :] * z_prev + cw_ref[1:2, :] * z + cw_ref[2:3, :] * z_next
    y = (gb_ref[0].astype(F32) * conv).astype(BF16)
    mix = jnp.dot(y, w_ref[...], preferred_element_type=F32)
    out_ref[0] = x_ref[0] + g1_ref[0] * mix


def _conv_out(z, gb, conv_w, w_out, x, mod3, gate_chunk):
    b, l, d = x.shape
    tm = 256
    hb = tm // SUBLANES
    n_halo = l // SUBLANES
    row = lambda bi, i: (bi, i, 0)
    return pl.pallas_call(
        _conv_out_kernel,
        out_shape=jax.ShapeDtypeStruct((b, l, d), F32),
        grid=(b, l // tm),
        in_specs=[
            pl.BlockSpec((1, tm, d), row),
            pl.BlockSpec((1, SUBLANES, d), lambda bi, i: (bi, jnp.maximum(i * hb - 1, 0), 0)),
            pl.BlockSpec((1, SUBLANES, d),
                         lambda bi, i: (bi, jnp.minimum((i + 1) * hb, n_halo - 1), 0)),
            pl.BlockSpec((1, tm, d), row),
            pl.BlockSpec(conv_w.shape, lambda bi, i: (0, 0)),
            pl.BlockSpec(w_out.shape, lambda bi, i: (0, 0)),
            pl.BlockSpec((1, tm, d), row),
            pl.BlockSpec((1, 1, d), lambda bi, i: (bi, 0, gate_chunk)),
        ],
        out_specs=pl.BlockSpec((1, tm, d), row),
        compiler_params=_cparams("arbitrary", "arbitrary"),
        name="conv_out",
    )(z, z, z, gb, conv_w, w_out, x, mod3)


def _pack_rows(h, hp_ref):
    tm, d = h.shape
    bits = pltpu.bitcast(h.astype(BF16).astype(F32), jnp.uint32)
    words = (bits[:, :d // 2] >> 16) | (bits[:, d // 2:] & jnp.uint32(0xFFFF0000))
    for s in range(ROW_WORDS):
        hp_ref[pl.ds(s, tm, stride=ROW_WORDS), :] = words[:, s * LANES:(s + 1) * LANES]


def _unpack_rows(buf_ref, n_rows):
    lows, highs = [], []
    for s in range(ROW_WORDS):
        w = buf_ref[pl.ds(s, n_rows, stride=ROW_WORDS), :]
        lows.append(pltpu.bitcast(w << 16, F32).astype(BF16))
        highs.append(pltpu.bitcast(w & jnp.uint32(0xFFFF0000), F32).astype(BF16))
    return jnp.concatenate(lows + highs, axis=-1)


def _router_kernel(x_ref, g_ref, sh_ref, sc_ref, whi_ref, wlo_ref, br_ref, hp_ref, idx_ref,
                   gate_ref):
    h = _modulate(x_ref[0], g_ref[...], sh_ref[0], sc_ref[0])
    _pack_rows(h, hp_ref)
    h_hi = h.astype(BF16)
    h_lo = (h - h_hi.astype(F32)).astype(BF16)
    logits = (jnp.dot(h_hi, whi_ref[...], preferred_element_type=F32)
              + (jnp.dot(h_lo, whi_ref[...], preferred_element_type=F32)
                 + jnp.dot(h_hi, wlo_ref[...], preferred_element_type=F32))) + br_ref[...]
    lane = lax.broadcasted_iota(jnp.int32, logits.shape, 1).astype(F32)
    work = jnp.where(lane < N_EXPERTS, logits, -jnp.inf)
    vals, idxs = [], []
    for _ in range(TOP_K):
        mx = jnp.max(work, axis=-1, keepdims=True)
        ix = jnp.min(jnp.where(work == mx, lane, float(LANES)), axis=-1, keepdims=True)
        vals.append(mx)
        idxs.append(ix.astype(jnp.int32))
        work = jnp.where(lane == ix, -jnp.inf, work)
    ex = [jnp.exp(v - vals[0]) for v in vals]
    den = ex[0] + ex[1] + ex[2] + ex[3]
    for k in range(TOP_K):
        idx_ref[:, k:k + 1] = idxs[k]
        gate_ref[:, k:k + 1] = ex[k] / den


def _router(x, g, mod3, chunk0, w_r, b_r):
    b, l, d = x.shape
    w_hi = w_r.astype(BF16)
    w_lo = (w_r - w_hi.astype(F32)).astype(BF16)
    tm = 512
    nt = l // tm
    t = b * l
    return pl.pallas_call(
        _router_kernel,
        out_shape=(jax.ShapeDtypeStruct((t * ROW_WORDS, LANES), jnp.uint32),
                   jax.ShapeDtypeStruct((t, TOP_K), jnp.int32),
                   jax.ShapeDtypeStruct((t, TOP_K), F32)),
        grid=(b, nt),
        in_specs=[
            pl.BlockSpec((1, tm, d), lambda bi, i: (bi, i, 0)),
            pl.BlockSpec((1, d), lambda bi, i: (0, 0)),
            pl.BlockSpec((1, 1, d), lambda bi, i: (bi, 0, chunk0)),
            pl.BlockSpec((1, 1, d), lambda bi, i: (bi, 0, chunk0 + 1)),
            pl.BlockSpec((d, LANES), lambda bi, i: (0, 0)),
            pl.BlockSpec((d, LANES), lambda bi, i: (0, 0)),
            pl.BlockSpec((1, LANES), lambda bi, i: (0, 0)),
        ],
        out_specs=(pl.BlockSpec((tm * ROW_WORDS, LANES), lambda bi, i: (bi * nt + i, 0)),
                   pl.BlockSpec((tm, TOP_K), lambda bi, i: (bi * nt + i, 0)),
                   pl.BlockSpec((tm, TOP_K), lambda bi, i: (bi * nt + i, 0))),
        compiler_params=_cparams("arbitrary", "arbitrary"),
        name="router",
    )(x, g, mod3, mod3, w_hi, w_lo, b_r)


def _issue_row_copies(n_rows, make_copy):
    def issue(it, carry):
        for u in range(DMA_UNROLL):
            make_copy(it * DMA_UNROLL + u).start(priority=u % 2)
        return carry

    lax.fori_loop(0, n_rows // DMA_UNROLL, issue, 0)


def _gather_kernel(nu_ref, tok_ref, tokn_ref, hp_ref, xs_ref, buf_ref, sem):
    i = pl.program_id(0)
    n_used = nu_ref[0]
    slot = lax.rem(i, 2)

    def issue_from(t_ref, dst_slot):
        def make_copy(r):
            src = pl.multiple_of(t_ref[0, 0, r] * ROW_WORDS, ROW_WORDS)
            return pltpu.make_async_copy(hp_ref.at[pl.ds(src, ROW_WORDS)],
                                         buf_ref.at[dst_slot, pl.ds(r * ROW_WORDS, ROW_WORDS)],
                                         sem.at[dst_slot])
        _issue_row_copies(GATHER_ROWS, make_copy)

    @pl.when(jnp.logical_and(i == 0, n_used > 0))
    def _():
        issue_from(tok_ref, 0)

    @pl.when(i + 1 < n_used)
    def _():
        issue_from(tokn_ref, 1 - slot)

    @pl.when(i < n_used)
    def _():
        pltpu.make_async_copy(hp_ref.at[pl.ds(0, GATHER_ROWS * ROW_WORDS)], buf_ref.at[slot],
                              sem.at[slot]).wait()
        xs_ref[...] = _unpack_rows(buf_ref.at[slot], GATHER_ROWS)

    @pl.when(i >= n_used)
    def _():
        xs_ref[...] = jnp.zeros_like(xs_ref)


def _gather_rows(n_used, slot_tok, hp, d):
    n_slots = slot_tok.shape[0]
    n_steps = n_slots // GATHER_ROWS
    tok3 = slot_tok.reshape(n_steps, 1, GATHER_ROWS)
    return pl.pallas_call(
        _gather_kernel,
        out_shape=jax.ShapeDtypeStruct((n_slots, d), BF16),
        grid_spec=pltpu.PrefetchScalarGridSpec(
            num_scalar_prefetch=1,
            grid=(n_steps,),
            in_specs=[
                pl.BlockSpec((1, 1, GATHER_ROWS), lambda i, nu: (i, 0, 0), memory_space=pltpu.SMEM),
                pl.BlockSpec((1, 1, GATHER_ROWS),
                             lambda i, nu: (jnp.minimum(i + 1, n_steps - 1), 0, 0),
                             memory_space=pltpu.SMEM),
                pl.BlockSpec(memory_space=pl.ANY),
            ],
            out_specs=pl.BlockSpec((GATHER_ROWS, d), lambda i, nu: (i, 0)),
            scratch_shapes=[pltpu.VMEM((2, GATHER_ROWS * ROW_WORDS, LANES), jnp.uint32),
                            pltpu.SemaphoreType.DMA((2,))],
        ),
        compiler_params=_cparams("arbitrary"),
        name="gather_rows",
    )(n_used, tok3, tok3, hp)


def _block_state(be_ref):
    bi = pl.program_id(1)
    n_valid = be_ref[pl.num_programs(1) + bi]
    prev = be_ref[jnp.maximum(bi - 1, 0)]
    return n_valid, jnp.logical_or(bi == 0, be_ref[bi] != prev)


def _when_rows(n_valid, tm, compute, zero_from):
    half = tm // 2

    @pl.when(n_valid > half)
    def _():
        compute(tm)

    @pl.when(jnp.logical_and(n_valid > 0, n_valid <= half))
    def _():
        compute(half)
        zero_from(half)

    @pl.when(n_valid == 0)
    def _():
        zero_from(0)


def _gmm_gu_kernel(be_ref, x_ref, wg_ref, wl_ref, bg_ref, bl_ref, act_ref, wg_s, wl_s):
    n_valid, changed = _block_state(be_ref)
    tm, tn = act_ref.shape

    @pl.when(jnp.logical_and(n_valid > 0, changed))
    def _():
        wg_s[...] = wg_ref[...].astype(BF16)
        wl_s[...] = wl_ref[...].astype(BF16)

    def compute(rows):
        xb = x_ref[0:rows, :]
        glu = jnp.dot(xb, wg_s[...], preferred_element_type=F32) + bg_ref[...]
        lin = jnp.dot(xb, wl_s[...], preferred_element_type=F32) + bl_ref[...]
        glu = jnp.minimum(glu, SWIGLU_LIMIT)
        lin = jnp.clip(lin, -SWIGLU_LIMIT, SWIGLU_LIMIT)
        sig = 1.0 / (1.0 + jnp.exp(-SWIGLU_ALPHA * glu))
        act_ref[0:rows, :] = (glu * sig * (lin + 1.0)).astype(BF16)

    def zero_from(r):
        act_ref[r:, :] = jnp.zeros((tm - r, tn), BF16)

    _when_rows(n_valid, tm, compute, zero_from)


def _gmm_gu(block_e, xs, w_gu, b_gu4, layer):
    n_slots, d = xs.shape
    f = w_gu.shape[3] // 2
    tm, tn = EXPERT_TM, 1024
    nn = f // tn
    return pl.pallas_call(
        _gmm_gu_kernel,
        out_shape=jax.ShapeDtypeStruct((n_slots, f), BF16),
        grid_spec=pltpu.PrefetchScalarGridSpec(
            num_scalar_prefetch=1,
            grid=(nn, n_slots // tm),
            in_specs=[
                pl.BlockSpec((tm, d), lambda j, i, be: (i, 0)),
                pl.BlockSpec((None, None, d, tn), lambda j, i, be: (layer, be[i], 0, j)),
                pl.BlockSpec((None, None, d, tn), lambda j, i, be: (layer, be[i], 0, nn + j)),
                pl.BlockSpec((None, None, 1, tn), lambda j, i, be: (layer, be[i], 0, j)),
                pl.BlockSpec((None, None, 1, tn), lambda j, i, be: (layer, be[i], 0, nn + j)),
            ],
            out_specs=pl.BlockSpec((tm, tn), lambda j, i, be: (i, j)),
            scratch_shapes=[pltpu.VMEM((d, tn), BF16), pltpu.VMEM((d, tn), BF16)],
        ),
        compiler_params=_cparams("arbitrary", "arbitrary"),
        name="gmm_gate_up",
    )(block_e, xs, w_gu, w_gu, b_gu4, b_gu4)


DOWN_TN = SUBLANES * LANES


def _gmm_down_kernel(be_ref, a_ref, w_ref, b_ref, y_ref, w_s):
    n_valid, changed = _block_state(be_ref)
    tm = y_ref.shape[0]

    @pl.when(jnp.logical_and(n_valid > 0, changed))
    def _():
        w_s[...] = w_ref[...].astype(BF16)

    def compute(rows):
        y = jnp.dot(a_ref[0:rows, :], w_s[...], preferred_element_type=F32) + b_ref[...]
        y2_ref = y_ref.reshape(tm * SUBLANES, LANES)
        for cc in range(SUBLANES):
            y2_ref[pl.ds(cc, rows, stride=SUBLANES), :] = y[:, cc * LANES:(cc + 1) * LANES]

    def zero_from(r):
        y_ref[r:, :, :] = jnp.zeros((tm - r, SUBLANES, LANES), F32)

    _when_rows(n_valid, tm, compute, zero_from)


def _gmm_down(block_e, act, w_down, b_down4, layer):
    n_slots, f = act.shape
    d = w_down.shape[3]
    tm, tn = EXPERT_TM, DOWN_TN
    return pl.pallas_call(
        _gmm_down_kernel,
        out_shape=jax.ShapeDtypeStruct((n_slots, d // LANES, LANES), F32),
        grid_spec=pltpu.PrefetchScalarGridSpec(
            num_scalar_prefetch=1,
            grid=(d // tn, n_slots // tm),
            in_specs=[
                pl.BlockSpec((tm, f), lambda j, i, be: (i, 0)),
                pl.BlockSpec((None, None, f, tn), lambda j, i, be: (layer, be[i], 0, j)),
                pl.BlockSpec((None, None, 1, tn), lambda j, i, be: (layer, be[i], 0, j)),
            ],
            out_specs=pl.BlockSpec((tm, SUBLANES, LANES), lambda j, i, be: (i, j, 0)),
            scratch_shapes=[pltpu.VMEM((f, tn), BF16)],
        ),
        compiler_params=_cparams("arbitrary", "arbitrary"),
        name="gmm_down",
    )(block_e, act, w_down, b_down4)


COMBINE_ROWS = COMBINE_TOK * TOP_K
COMBINE_SUB = 32


def _combine_kernel(pos_ref, posn_ref, ys_ref, gate_ref, x_ref, g2_ref, gf_ref, out_ref,
                    buf_ref, sem, *, final_norm):
    i = pl.program_id(0)
    n = pl.num_programs(0)
    slot = lax.rem(i, 2)
    n_chunks = x_ref.shape[1] // LANES

    def issue_from(p_ref, dst_slot):
        def make_copy(r):
            src = pl.multiple_of(p_ref[0, 0, r] * n_chunks, n_chunks)
            return pltpu.make_async_copy(ys_ref.at[pl.ds(src, n_chunks)],
                                         buf_ref.at[dst_slot, pl.ds(r * n_chunks, n_chunks)],
                                         sem.at[dst_slot])
        _issue_row_copies(COMBINE_ROWS, make_copy)

    @pl.when(i == 0)
    def _():
        issue_from(pos_ref, 0)

    @pl.when(i + 1 < n)
    def _():
        issue_from(posn_ref, 1 - slot)

    pltpu.make_async_copy(ys_ref.at[pl.ds(0, COMBINE_ROWS * n_chunks)], buf_ref.at[slot],
                          sem.at[slot]).wait()

    rows_ref = buf_ref.at[slot]
    d = n_chunks * LANES
    for t0 in range(0, COMBINE_TOK, COMBINE_SUB):
        tok = slice(t0, t0 + COMBINE_SUB)
        gates = [jnp.broadcast_to(gate_ref[tok, k:k + 1], (COMBINE_SUB, LANES)) for k in range(TOP_K)]
        ssq = jnp.zeros((COMBINE_SUB, LANES), F32)
        for c in range(n_chunks):
            cols = slice(c * LANES, (c + 1) * LANES)
            y = None
            for k in range(TOP_K):
                first = (k * COMBINE_TOK + t0) * n_chunks + c
                rows = rows_ref[pl.ds(first, COMBINE_SUB, stride=n_chunks), :]
                y = gates[k] * rows if y is None else y + gates[k] * rows
            o = x_ref[tok, cols] + g2_ref[0, :, cols] * y
            out_ref[tok, cols] = o
            ssq = ssq + o * o
        if final_norm:
            inv = lax.rsqrt(jnp.sum(ssq, axis=-1, keepdims=True) / d + RMS_EPS)
            out_ref[tok, :] = out_ref[tok, :] * inv * gf_ref[...]


def _combine(pos, ys2, gate, x2, mod3, gate_chunk, tokens_per_batch, final_g, final_norm):
    t, d = x2.shape
    n_chunks = d // LANES
    n_steps = t // COMBINE_TOK
    steps_per_batch = tokens_per_batch // COMBINE_TOK
    pos3 = pos.reshape(n_steps, COMBINE_TOK, TOP_K).transpose(0, 2, 1).reshape(n_steps, 1, COMBINE_ROWS)
    return pl.pallas_call(
        functools.partial(_combine_kernel, final_norm=final_norm),
        out_shape=jax.ShapeDtypeStruct((t, d), F32),
        grid=(n_steps,),
        in_specs=[
            pl.BlockSpec((1, 1, COMBINE_ROWS), lambda i: (i, 0, 0), memory_space=pltpu.SMEM),
            pl.BlockSpec((1, 1, COMBINE_ROWS), lambda i: (jnp.minimum(i + 1, n_steps - 1), 0, 0),
                         memory_space=pltpu.SMEM),
            pl.BlockSpec(memory_space=pl.ANY),
            pl.BlockSpec((COMBINE_TOK, TOP_K), lambda i: (i, 0)),
            pl.BlockSpec((COMBINE_TOK, d), lambda i: (i, 0)),
            pl.BlockSpec((1, 1, d), lambda i: (i // steps_per_batch, 0, gate_chunk)),
            pl.BlockSpec((1, d), lambda i: (0, 0)),
        ],
        out_specs=pl.BlockSpec((COMBINE_TOK, d), lambda i: (i, 0)),
        scratch_shapes=[pltpu.VMEM((2, COMBINE_ROWS * n_chunks, LANES), F32),
                        pltpu.SemaphoreType.DMA((2,))],
        compiler_params=_cparams("arbitrary"),
        name="combine",
    )(pos3, pos3, ys2, gate, x2, mod3, final_g)


SCATTER_CHUNK = 8192
TOP_K_SHIFT = 2
assert 1 << TOP_K_SHIFT == TOP_K and DMA_UNROLL % TOP_K == 0


def _slot_tok_kernel(pos_ref, zeros_ref, out_ref):
    i = pl.program_id(0)

    @pl.when(i == 0)
    def _():
        pltpu.sync_copy(zeros_ref, out_ref)

    base = i * SCATTER_CHUNK

    def scatter(it, carry):
        a0 = it * DMA_UNROLL
        tok0 = lax.shift_right_logical(base + a0, TOP_K_SHIFT)
        for u in range(DMA_UNROLL):
            out_ref[pos_ref[0, 0, a0 + u]] = tok0 + (u >> TOP_K_SHIFT)
        return carry

    lax.fori_loop(0, SCATTER_CHUNK // DMA_UNROLL, scatter, 0)


def _slot_tokens(pos, n_slots):
    n_steps = pos.shape[0] // SCATTER_CHUNK
    return pl.pallas_call(
        _slot_tok_kernel,
        out_shape=jax.ShapeDtypeStruct((n_slots,), jnp.int32),
        grid=(n_steps,),
        in_specs=[pl.BlockSpec((1, 1, SCATTER_CHUNK), lambda i: (i, 0, 0),
                               memory_space=pltpu.SMEM),
                  pl.BlockSpec(memory_space=pl.ANY)],
        out_specs=pl.BlockSpec(memory_space=pltpu.SMEM),
        compiler_params=_cparams("arbitrary"),
        name="slot_tokens",
    )(pos.reshape(n_steps, 1, SCATTER_CHUNK), jnp.zeros((n_slots,), jnp.int32))


def _slot_plan(top_idx):
    t = top_idx.shape[0]
    n_assign = t * TOP_K
    flat_e = top_idx.reshape(-1)
    onehot = (flat_e[:, None] == jnp.arange(N_EXPERTS, dtype=jnp.int32)[None, :]).astype(jnp.int32)
    csum = jnp.cumsum(onehot, axis=0)
    counts = csum[-1]
    rank = jnp.sum(csum * onehot, axis=1) - 1
    padded = (counts + EXPERT_TM - 1) // EXPERT_TM * EXPERT_TM
    padded_end = jnp.cumsum(padded)
    padded_start = padded_end - padded
    pos = (padded_start[flat_e] + rank).astype(jnp.int32)
    n_blocks = -(-(n_assign + N_EXPERTS * (EXPERT_TM - 1)) // EXPERT_TM)
    n_blocks = -(-n_blocks * EXPERT_TM // GATHER_ROWS) * GATHER_ROWS // EXPERT_TM
    n_slots = n_blocks * EXPERT_TM
    slot_tok = _slot_tokens(pos, n_slots)
    block_start = jnp.arange(n_blocks, dtype=jnp.int32) * EXPERT_TM
    block_e = jnp.minimum(jnp.sum(block_start[:, None] >= padded_end[None, :], axis=1),
                          N_EXPERTS - 1).astype(jnp.int32)
    n_used = (padded_end[-1:] // EXPERT_TM).astype(jnp.int32)
    real_end = padded_start + counts
    n_valid = jnp.clip(real_end[block_e] - block_start, 0, EXPERT_TM).astype(jnp.int32)
    return pos, slot_tok, jnp.concatenate([block_e, n_valid]), n_used


def _expert_ffn_residual(x, mod3, layer, norm_g, w_r, b_r, w_gu, b_gu4, w_down, b_down4,
                         final_g, final_norm):
    b, l, d = x.shape
    assert d == 2 * ROW_WORDS * LANES
    hp, top_idx, gate = _router(x, norm_g, mod3, 3, w_r, b_r)
    pos, slot_tok, block_e, n_used = _slot_plan(top_idx)
    xs = _gather_rows(n_used, slot_tok, hp, d)
    act = _gmm_gu(block_e, xs, w_gu, b_gu4, layer)
    ys3 = _gmm_down(block_e, act, w_down, b_down4, layer)
    ys2 = ys3.reshape(ys3.shape[0] * ys3.shape[1], LANES)
    out = _combine(pos, ys2, gate, x.reshape(b * l, d), mod3, 5, l, final_g, final_norm)
    return out.reshape(b, l, d)


def _rope_tables(length):
    rows = length // GRID_W
    row = jnp.repeat(jnp.arange(rows), GRID_W).astype(F32)
    col = jnp.tile(jnp.arange(GRID_W), rows).astype(F32)
    inv = 1.0 / (ROPE_THETA ** (jnp.arange(0, ROPE_AXIS, 2, dtype=F32) / ROPE_AXIS))
    ang_r = row[:, None] * inv[None, :]
    ang_c = col[:, None] * inv[None, :]
    cos = jnp.concatenate([jnp.cos(ang_r)] * 2 + [jnp.cos(ang_c)] * 2, axis=-1)
    sin = jnp.concatenate([jnp.sin(ang_r)] * 2 + [jnp.sin(ang_c)] * 2, axis=-1)
    return cos, sin


def _rotate_half_cols(w):
    a, b_, c_, d_ = jnp.split(w, 4, axis=-1)
    return jnp.concatenate([-b_, a, -d_, c_], axis=-1)


def kernel(x, c, ctx, c_ctx, ada_w, ada_b, norm_mix_g, norm_ffn_g, mla_w_in, mla_q_norm_g,
           mla_kv_norm_g, mla_w_q_up, mla_w_kv_up, mla_w_out, conv_w_in, conv_w, conv_w_out,
           router_w, router_b, expert_w_gu, expert_b_gu, expert_w_down, expert_b_down,
           final_norm_g):
    b, l, d = x.shape
    depth = ada_w.shape[0]
    lc = ctx.shape[1]
    assert depth == 2 and b + 1 <= ADA_ROWS

    cv = jnp.concatenate([c, c_ctx[None, :], jnp.zeros((ADA_ROWS - b - 1, d), F32)], axis=0)
    ctx_row = b
    ada_b3 = ada_b.reshape(depth, 1, N_ADA * d)
    cos, sin = _rope_tables(l)
    b_gu4 = expert_b_gu.reshape(depth, N_EXPERTS, 1, -1)
    b_down4 = expert_b_down.reshape(depth, N_EXPERTS, 1, d)
    w_r = jnp.pad(router_w, ((0, 0), (0, 0), (0, LANES - N_EXPERTS)))
    b_r = jnp.pad(router_b, ((0, 0), (0, LANES - N_EXPERTS)))[:, None, :]
    final_g = final_norm_g[None, :]

    mod3 = _ada(cv, ada_w, ada_b3, 0).reshape(ADA_ROWS, 1, N_ADA * d)
    w_in = mla_w_in[0]
    w_kr = w_in[:, Q_LORA + KV_LORA:]
    w_all = jnp.concatenate([w_in, _rotate_half_cols(w_kr)], axis=1).astype(BF16)
    w_ctx = w_in[:, Q_LORA:].astype(BF16)
    gq = mla_q_norm_g[0][None, :]
    gkv = mla_kv_norm_g[0][None, :]
    g_mix = norm_mix_g[0][None, :]
    wq = mla_w_q_up[0].reshape(Q_LORA, N_HEADS, QK_DIM)
    wq_rope = wq[..., QK_NOPE:]
    wq_all = jnp.concatenate([wq, _rotate_half_cols(wq_rope)], axis=-1)
    wq_all = wq_all.reshape(Q_LORA, N_HEADS * Q_HEAD_COLS).astype(BF16)
    wkv = mla_w_kv_up[0].astype(BF16)
    w_o = mla_w_out[0].astype(BF16)

    cq, ckv, kr = _mla_in(x, g_mix, mod3, 0, w_all, gq, gkv, cos, sin)
    ckv_c, kr_c = _mla_in_ctx(ctx, g_mix, mod3, ctx_row, w_ctx, gkv)
    ckv_all = jnp.concatenate([ckv_c, ckv], axis=1)
    kr_all = jnp.concatenate([kr_c, kr], axis=1)
    q = _q_up(cq, wq_all, cos, sin)
    k, v = _kv_up(ckv_all, kr_all, wkv, 768)
    o = _attention(q, k, v, 1024, 768)
    x = _proj_res(o, w_o, x, mod3, 2)
    x = _expert_ffn_residual(x, mod3, 0, norm_ffn_g[0][None, :], w_r[0], b_r[0], expert_w_gu,
                             b_gu4, expert_w_down, b_down4, final_g, False)

    mod3 = _ada(cv, ada_w, ada_b3, 1).reshape(ADA_ROWS, 1, N_ADA * d)
    gb, z = _conv_in(x, norm_mix_g[1][None, :], mod3, 0, conv_w_in[0].astype(BF16))
    x = _conv_out(z, gb, conv_w[0], conv_w_out[0].astype(BF16), x, mod3, 2)
    x = _expert_ffn_residual(x, mod3, 1, norm_ffn_g[1][None, :], w_r[1], b_r[1], expert_w_gu,
                             b_gu4, expert_w_down, b_down4, final_g, True)
    return x
```

```python
import functools

import jax
import jax.numpy as jnp
import numpy as np
from jax import lax
from jax.experimental import pallas as pl
from jax.experimental.pallas import tpu as pltpu

GRID_W = 64
RMS_EPS = 1e-6
N_ADA = 6
N_HEADS = 16
Q_LORA = 512
KV_LORA = 512
QK_NOPE = 128
QK_ROPE = 64
V_DIM = 128
ROPE_AXIS = QK_ROPE // 2
ROPE_THETA = 10000.0
QK_DIM = QK_NOPE + QK_ROPE
ATTN_SCALE = QK_DIM ** -0.5
LOG2_E = 1.4426950408889634
Q_SCALE = ATTN_SCALE * LOG2_E
N_EXPERTS = 32
TOP_K = 4
SWIGLU_LIMIT = 7.0
SWIGLU_ALPHA = 1.702

LANES = 128
SUBLANES = 8
VMEM_LIMIT = 56 * 1024 * 1024

ADA_ROWS = 8
EXPERT_TM = 512
GATHER_ROWS = EXPERT_TM
COMBINE_TOK = 128
ROW_WORDS = SUBLANES
DMA_UNROLL = 8

BF16 = jnp.bfloat16
F32 = jnp.float32


def _cparams(*sem):
    return pltpu.CompilerParams(dimension_semantics=sem, vmem_limit_bytes=VMEM_LIMIT)


def _rms(x):
    return x * lax.rsqrt(jnp.mean(x * x, axis=-1, keepdims=True) + RMS_EPS)


def _modulate(x, g, shift, scale):
    return _rms(x) * g * (1.0 + scale) + shift


def _ada_kernel(cv_ref, w_ref, b_ref, o_ref):
    cv = cv_ref[...]
    s = cv * (1.0 / (1.0 + jnp.exp(-cv)))
    o_ref[...] = jnp.dot(s, w_ref[...], preferred_element_type=F32,
                         precision=lax.Precision.HIGHEST) + b_ref[...]


def _ada(cv, ada_w, ada_b3, layer):
    d = cv.shape[1]
    n = ada_w.shape[2]
    tn = 1024
    return pl.pallas_call(
        _ada_kernel,
        out_shape=jax.ShapeDtypeStruct((ADA_ROWS, n), F32),
        grid=(n // tn,),
        in_specs=[
            pl.BlockSpec((ADA_ROWS, d), lambda j: (0, 0)),
            pl.BlockSpec((None, d, tn), lambda j: (layer, 0, j)),
            pl.BlockSpec((None, 1, tn), lambda j: (layer, 0, j)),
        ],
        out_specs=pl.BlockSpec((ADA_ROWS, tn), lambda j: (0, j)),
        compiler_params=_cparams("arbitrary"),
        name="ada",
    )(cv, ada_w, ada_b3)


def _mla_in_kernel(x_ref, g_ref, sh_ref, sc_ref, w_ref, gq_ref, gkv_ref, cos_ref, sin_ref,
                   cq_ref, ckv_ref, kr_ref):
    h = _modulate(x_ref[0], g_ref[...], sh_ref[0], sc_ref[0]).astype(BF16)
    p = jnp.dot(h, w_ref[...], preferred_element_type=F32)
    cq_ref[0] = (_rms(p[:, :Q_LORA]) * gq_ref[...]).astype(BF16)
    ckv_ref[0] = (_rms(p[:, Q_LORA:Q_LORA + KV_LORA]) * gkv_ref[...]).astype(BF16)
    o = Q_LORA + KV_LORA
    kr = p[:, o:o + QK_ROPE]
    kr_rot = p[:, o + QK_ROPE:o + 2 * QK_ROPE]
    kr_ref[0] = (kr * cos_ref[...] + kr_rot * sin_ref[...]).astype(BF16)


def _mla_in_ctx_kernel(x_ref, g_ref, sh_ref, sc_ref, w_ref, gkv_ref, ckv_ref, kr_ref):
    h = _modulate(x_ref[0], g_ref[...], sh_ref[0], sc_ref[0]).astype(BF16)
    p = jnp.dot(h, w_ref[...], preferred_element_type=F32)
    ckv_ref[0] = (_rms(p[:, :KV_LORA]) * gkv_ref[...]).astype(BF16)
    kr_ref[0] = p[:, KV_LORA:KV_LORA + QK_ROPE].astype(BF16)


def _mla_in(x, g, mod3, layer_chunk0, w_all, gq, gkv, cos, sin):
    b, l, d = x.shape
    tm = 512
    nw = w_all.shape[1]
    row = lambda bi, i: (bi, i, 0)
    return pl.pallas_call(
        _mla_in_kernel,
        out_shape=(jax.ShapeDtypeStruct((b, l, Q_LORA), BF16),
                   jax.ShapeDtypeStruct((b, l, KV_LORA), BF16),
                   jax.ShapeDtypeStruct((b, l, QK_ROPE), BF16)),
        grid=(b, l // tm),
        in_specs=[
            pl.BlockSpec((1, tm, d), row),
            pl.BlockSpec((1, d), lambda bi, i: (0, 0)),
            pl.BlockSpec((1, 1, d), lambda bi, i: (bi, 0, layer_chunk0)),
            pl.BlockSpec((1, 1, d), lambda bi, i: (bi, 0, layer_chunk0 + 1)),
            pl.BlockSpec((d, nw), lambda bi, i: (0, 0)),
            pl.BlockSpec((1, Q_LORA), lambda bi, i: (0, 0)),
            pl.BlockSpec((1, KV_LORA), lambda bi, i: (0, 0)),
            pl.BlockSpec((tm, QK_ROPE), lambda bi, i: (i, 0)),
            pl.BlockSpec((tm, QK_ROPE), lambda bi, i: (i, 0)),
        ],
        out_specs=(pl.BlockSpec((1, tm, Q_LORA), row),
                   pl.BlockSpec((1, tm, KV_LORA), row),
                   pl.BlockSpec((1, tm, QK_ROPE), row)),
        compiler_params=_cparams("arbitrary", "arbitrary"),
        name="mla_in",
    )(x, g, mod3, mod3, w_all, gq, gkv, cos, sin)


def _mla_in_ctx(ctx, g, mod3, ctx_row, w_kv, gkv):
    b, l, d = ctx.shape
    tm = l
    nw = w_kv.shape[1]
    row = lambda bi, i: (bi, i, 0)
    return pl.pallas_call(
        _mla_in_ctx_kernel,
        out_shape=(jax.ShapeDtypeStruct((b, l, KV_LORA), BF16),
                   jax.ShapeDtypeStruct((b, l, QK_ROPE), BF16)),
        grid=(b, l // tm),
        in_specs=[
            pl.BlockSpec((1, tm, d), row),
            pl.BlockSpec((1, d), lambda bi, i: (0, 0)),
            pl.BlockSpec((1, 1, d), lambda bi, i: (ctx_row, 0, 0)),
            pl.BlockSpec((1, 1, d), lambda bi, i: (ctx_row, 0, 1)),
            pl.BlockSpec((d, nw), lambda bi, i: (0, 0)),
            pl.BlockSpec((1, KV_LORA), lambda bi, i: (0, 0)),
        ],
        out_specs=(pl.BlockSpec((1, tm, KV_LORA), row),
                   pl.BlockSpec((1, tm, QK_ROPE), row)),
        compiler_params=_cparams("arbitrary", "arbitrary"),
        name="mla_in_ctx",
    )(ctx, g, mod3, mod3, w_kv, gkv)


Q_HEAD_COLS = QK_NOPE + 2 * QK_ROPE
KV_HEAD_COLS = QK_NOPE + V_DIM


def _q_up_kernel(cq_ref, w_ref, cos_ref, sin_ref, q_ref):
    cq = cq_ref[0]
    cos = cos_ref[...]
    sin = sin_ref[...]
    for h in range(N_HEADS):
        p = jnp.dot(cq, w_ref[:, h * Q_HEAD_COLS:(h + 1) * Q_HEAD_COLS],
                    preferred_element_type=F32)
        rope = p[:, QK_NOPE:QK_NOPE + QK_ROPE] * cos + p[:, QK_NOPE + QK_ROPE:] * sin
        q_ref[0, h, :, :QK_NOPE] = (p[:, :QK_NOPE] * Q_SCALE).astype(BF16)
        q_ref[0, h, :, QK_NOPE:] = (rope * Q_SCALE).astype(BF16)


def _q_up(cq, w_q, cos, sin):
    b, l, _ = cq.shape
    tm = 512
    return pl.pallas_call(
        _q_up_kernel,
        out_shape=jax.ShapeDtypeStruct((b, N_HEADS, l, QK_DIM), BF16),
        grid=(b, l // tm),
        in_specs=[
            pl.BlockSpec((1, tm, Q_LORA), lambda bi, i: (bi, i, 0)),
            pl.BlockSpec(w_q.shape, lambda bi, i: (0, 0)),
            pl.BlockSpec((tm, QK_ROPE), lambda bi, i: (i, 0)),
            pl.BlockSpec((tm, QK_ROPE), lambda bi, i: (i, 0)),
        ],
        out_specs=pl.BlockSpec((1, N_HEADS, tm, QK_DIM), lambda bi, i: (bi, 0, i, 0)),
        compiler_params=_cparams("arbitrary", "arbitrary"),
        name="q_up",
    )(cq, w_q, cos, sin)


def _kv_up_kernel(ckv_ref, kr_ref, w_ref, k_ref, v_ref):
    ckv = ckv_ref[0]
    kr = kr_ref[0]
    for h in range(N_HEADS):
        p = jnp.dot(ckv, w_ref[:, h * KV_HEAD_COLS:(h + 1) * KV_HEAD_COLS],
                    preferred_element_type=F32)
        k_ref[0, h, :, :QK_NOPE] = p[:, :QK_NOPE].astype(BF16)
        k_ref[0, h, :, QK_NOPE:] = kr
        v_ref[0, h] = p[:, QK_NOPE:].astype(BF16)


def _kv_up(ckv, kr, w_kv, tm):
    b, l, _ = ckv.shape
    return pl.pallas_call(
        _kv_up_kernel,
        out_shape=(jax.ShapeDtypeStruct((b, N_HEADS, l, QK_DIM), BF16),
                   jax.ShapeDtypeStruct((b, N_HEADS, l, V_DIM), BF16)),
        grid=(b, l // tm),
        in_specs=[
            pl.BlockSpec((1, tm, KV_LORA), lambda bi, i: (bi, i, 0)),
            pl.BlockSpec((1, tm, QK_ROPE), lambda bi, i: (bi, i, 0)),
            pl.BlockSpec(w_kv.shape, lambda bi, i: (0, 0)),
        ],
        out_specs=(pl.BlockSpec((1, N_HEADS, tm, QK_DIM), lambda bi, i: (bi, 0, i, 0)),
                   pl.BlockSpec((1, N_HEADS, tm, V_DIM), lambda bi, i: (bi, 0, i, 0))),
        compiler_params=_cparams("arbitrary", "arbitrary"),
        name="kv_up",
    )(ckv, kr, w_kv)


def _attn_kernel(q_ref, k_ref, v_ref, o_ref, *, tk, n_chunks):
    q = q_ref[0, 0]
    tq = q.shape[0]
    n_slabs = tk // LANES
    m = jnp.full((tq, LANES), -jnp.inf, F32)
    l_loc = jnp.zeros((tq, LANES), F32)
    acc = jnp.zeros((tq, V_DIM), F32)
    for c in range(n_chunks):
        k = k_ref[0, 0, c * tk:(c + 1) * tk, :]
        s = lax.dot_general(q, k, (((1,), (1,)), ((), ())), preferred_element_type=F32)
        slabs = [s[:, j * LANES:(j + 1) * LANES] for j in range(n_slabs)]
        m_loc = slabs[0]
        for slab in slabs[1:]:
            m_loc = jnp.maximum(m_loc, slab)
        m_new = jnp.maximum(m, jnp.max(m_loc, axis=-1, keepdims=True))
        alpha = jnp.exp2(m - m_new)
        ps = [jnp.exp2(slab - m_new) for slab in slabs]
        p_sum = ps[0]
        for p in ps[1:]:
            p_sum = p_sum + p
        l_loc = alpha * l_loc + p_sum
        p_bf = jnp.concatenate([p.astype(BF16) for p in ps], axis=-1)
        acc = alpha * acc + jnp.dot(p_bf, v_ref[0, 0, c * tk:(c + 1) * tk, :],
                                    preferred_element_type=F32)
        m = m_new
    o_ref[0] = (acc / jnp.sum(l_loc, axis=-1, keepdims=True)).astype(BF16)


def _attention(q, k, v, tq, tk):
    b, h, l, _ = q.shape
    lk = k.shape[2]
    return pl.pallas_call(
        functools.partial(_attn_kernel, tk=tk, n_chunks=lk // tk),
        out_shape=jax.ShapeDtypeStruct((b, l, h * V_DIM), BF16),
        grid=(b, h, l // tq),
        in_specs=[
            pl.BlockSpec((1, 1, tq, QK_DIM), lambda bi, hi, i: (bi, hi, i, 0)),
            pl.BlockSpec((1, 1, lk, QK_DIM), lambda bi, hi, i: (bi, hi, 0, 0)),
            pl.BlockSpec((1, 1, lk, V_DIM), lambda bi, hi, i: (bi, hi, 0, 0)),
        ],
        out_specs=pl.BlockSpec((1, tq, V_DIM), lambda bi, hi, i: (bi, i, hi)),
        compiler_params=_cparams("arbitrary", "arbitrary", "arbitrary"),
        name="attention",
    )(q, k, v)


def _proj_res_kernel(o_ref, w_ref, x_ref, g1_ref, *refs):
    route_in, (out_ref, *route_out) = refs[:6], refs[6:]
    mix = jnp.dot(o_ref[0], w_ref[...], preferred_element_type=F32)
    x1 = x_ref[0] + g1_ref[0] * mix
    out_ref[0] = x1
    _route_tile(x1, *route_in, *route_out)


def _proj_res(o, w_out, x, mod3, gate_chunk, norm_ffn_g, w_r, b_r):
    b, l, d = x.shape
    tm = 512
    row = lambda bi, i: (bi, i, 0)
    r_in, r_specs, r_shapes, r_out_specs = _route_specs(b, l, d, tm, norm_ffn_g, mod3, 3, w_r, b_r)
    x1, *routed = pl.pallas_call(
        _proj_res_kernel,
        out_shape=(jax.ShapeDtypeStruct((b, l, d), F32), *r_shapes),
        grid=(b, l // tm),
        in_specs=[
            pl.BlockSpec((1, tm, o.shape[2]), row),
            pl.BlockSpec(w_out.shape, lambda bi, i: (0, 0)),
            pl.BlockSpec((1, tm, d), row),
            pl.BlockSpec((1, 1, d), lambda bi, i: (bi, 0, gate_chunk)),
            *r_specs,
        ],
        out_specs=(pl.BlockSpec((1, tm, d), row), *r_out_specs),
        compiler_params=_cparams("arbitrary", "arbitrary"),
        name="proj_res",
    )(o, w_out, x, mod3, *r_in)
    return x1, routed


def _conv_in_kernel(x_ref, g_ref, sh_ref, sc_ref, wb_ref, wc_ref, wu_ref, gb_ref, z_ref, h_ref):
    @pl.when(pl.program_id(2) == 0)
    def _():
        h_ref[...] = _modulate(x_ref[0], g_ref[...], sh_ref[0], sc_ref[0]).astype(BF16)

    h = h_ref[...]
    gb_ref[0] = jnp.dot(h, wb_ref[...], preferred_element_type=F32).astype(BF16)
    gc = jnp.dot(h, wc_ref[...], preferred_element_type=F32)
    u = jnp.dot(h, wu_ref[...], preferred_element_type=F32)
    z_ref[0] = (gc * u).astype(BF16)


def _conv_in(x, g, mod3, chunk0, w_in):
    b, l, d = x.shape
    tm, tn = 512, 1024
    nn = d // tn
    row = lambda bi, i, j: (bi, i, 0)
    col = lambda bi, i, j: (bi, i, j)
    return pl.pallas_call(
        _conv_in_kernel,
        out_shape=(jax.ShapeDtypeStruct((b, l, d), BF16), jax.ShapeDtypeStruct((b, l, d), BF16)),
        grid=(b, l // tm, nn),
        in_specs=[
            pl.BlockSpec((1, tm, d), row),
            pl.BlockSpec((1, d), lambda bi, i, j: (0, 0)),
            pl.BlockSpec((1, 1, d), lambda bi, i, j: (bi, 0, chunk0)),
            pl.BlockSpec((1, 1, d), lambda bi, i, j: (bi, 0, chunk0 + 1)),
            pl.BlockSpec((d, tn), lambda bi, i, j: (0, j)),
            pl.BlockSpec((d, tn), lambda bi, i, j: (0, nn + j)),
            pl.BlockSpec((d, tn), lambda bi, i, j: (0, 2 * nn + j)),
        ],
        out_specs=(pl.BlockSpec((1, tm, tn), col), pl.BlockSpec((1, tm, tn), col)),
        scratch_shapes=[pltpu.VMEM((tm, d), BF16)],
        compiler_params=_cparams("arbitrary", "arbitrary", "arbitrary"),
        name="conv_in",
    )(x, g, mod3, mod3, w_in, w_in, w_in)


def _conv_out_kernel(z_ref, zp_ref, zn_ref, gb_ref, cw_ref, w_ref, x_ref, g1_ref, *refs):
    route_in, (out_ref, *route_out) = refs[:6], refs[6:]
    i = pl.program_id(1)
    last = pl.num_programs(1) - 1
    z = z_ref[0].astype(F32)
    tm = z.shape[0]
    prev_row = jnp.where(i > 0, zp_ref[0, SUBLANES - 1:SUBLANES, :].astype(F32), 0.0)
    next_row = jnp.where(i < last, zn_ref[0, 0:1, :].astype(F32), 0.0)
    ridx = lax.broadcasted_iota(jnp.int32, z.shape, 0)
    z_prev = jnp.where(ridx == 0, prev_row, pltpu.roll(z, 1, 0))
    z_next = jnp.where(ridx == tm - 1, next_row, pltpu.roll(z, tm - 1, 0))
    conv = cw_ref[0:1, :] * z_prev + cw_ref[1:2, :] * z + cw_ref[2:3, :] * z_next
    y = (gb_ref[0].astype(F32) * conv).astype(BF16)
    mix = jnp.dot(y, w_ref[...], preferred_element_type=F32)
    x1 = x_ref[0] + g1_ref[0] * mix
    out_ref[0] = x1
    _route_tile(x1, *route_in, *route_out)


def _conv_out(z, gb, conv_w, w_out, x, mod3, gate_chunk, norm_ffn_g, w_r, b_r):
    b, l, d = x.shape
    tm = 256
    hb = tm // SUBLANES
    n_halo = l // SUBLANES
    row = lambda bi, i: (bi, i, 0)
    r_in, r_specs, r_shapes, r_out_specs = _route_specs(b, l, d, tm, norm_ffn_g, mod3, 3, w_r, b_r)
    x1, *routed = pl.pallas_call(
        _conv_out_kernel,
        out_shape=(jax.ShapeDtypeStruct((b, l, d), F32), *r_shapes),
        grid=(b, l // tm),
        in_specs=[
            pl.BlockSpec((1, tm, d), row),
            pl.BlockSpec((1, SUBLANES, d), lambda bi, i: (bi, jnp.maximum(i * hb - 1, 0), 0)),
            pl.BlockSpec((1, SUBLANES, d),
                         lambda bi, i: (bi, jnp.minimum((i + 1) * hb, n_halo - 1), 0)),
            pl.BlockSpec((1, tm, d), row),
            pl.BlockSpec(conv_w.shape, lambda bi, i: (0, 0)),
            pl.BlockSpec(w_out.shape, lambda bi, i: (0, 0)),
            pl.BlockSpec((1, tm, d), row),
            pl.BlockSpec((1, 1, d), lambda bi, i: (bi, 0, gate_chunk)),
            *r_specs,
        ],
        out_specs=(pl.BlockSpec((1, tm, d), row), *r_out_specs),
        compiler_params=_cparams("arbitrary", "arbitrary"),
        name="conv_out",
    )(z, z, z, gb, conv_w, w_out, x, mod3, *r_in)
    return x1, routed


def _pack_rows(h, hp_ref):
    tm, d = h.shape
    bits = pltpu.bitcast(h.astype(BF16).astype(F32), jnp.uint32)
    words = (bits[:, :d // 2] >> 16) | (bits[:, d // 2:] & jnp.uint32(0xFFFF0000))
    for s in range(ROW_WORDS):
        hp_ref[pl.ds(s, tm, stride=ROW_WORDS), :] = words[:, s * LANES:(s + 1) * LANES]


def _unpack_rows(buf_ref, n_rows):
    lows, highs = [], []
    for s in range(ROW_WORDS):
        w = buf_ref[pl.ds(s, n_rows, stride=ROW_WORDS), :]
        lows.append(pltpu.bitcast(w << 16, F32).astype(BF16))
        highs.append(pltpu.bitcast(w & jnp.uint32(0xFFFF0000), F32).astype(BF16))
    return jnp.concatenate(lows + highs, axis=-1)


def _route_tile(x, g_ref, sh_ref, sc_ref, whi_ref, wlo_ref, br_ref, hp_ref, idx_ref, gate_ref):
    h = _modulate(x, g_ref[...], sh_ref[0], sc_ref[0])
    _pack_rows(h, hp_ref)
    h_hi = h.astype(BF16)
    h_lo = (h - h_hi.astype(F32)).astype(BF16)
    logits = (jnp.dot(h_hi, whi_ref[...], preferred_element_type=F32)
              + (jnp.dot(h_lo, whi_ref[...], preferred_element_type=F32)
                 + jnp.dot(h_hi, wlo_ref[...], preferred_element_type=F32))) + br_ref[...]
    lane = lax.broadcasted_iota(jnp.int32, logits.shape, 1).astype(F32)
    work = jnp.where(lane < N_EXPERTS, logits, -jnp.inf)
    vals, idxs = [], []
    for _ in range(TOP_K):
        mx = jnp.max(work, axis=-1, keepdims=True)
        ix = jnp.min(jnp.where(work == mx, lane, float(LANES)), axis=-1, keepdims=True)
        vals.append(mx)
        idxs.append(ix.astype(jnp.int32))
        work = jnp.where(lane == ix, -jnp.inf, work)
    ex = [jnp.exp(v - vals[0]) for v in vals]
    den = ex[0] + ex[1] + ex[2] + ex[3]
    for k in range(TOP_K):
        idx_ref[:, k:k + 1] = idxs[k]
        gate_ref[:, k:k + 1] = ex[k] / den


def _route_specs(b, l, d, tm, norm_g, mod3, chunk0, w_r, b_r):
    nt = l // tm
    t = b * l
    w_hi = w_r.astype(BF16)
    w_lo = (w_r - w_hi.astype(F32)).astype(BF16)
    flat = lambda bi, i: (bi * nt + i, 0)
    const = lambda bi, i: (0, 0)
    inputs = (norm_g, mod3, mod3, w_hi, w_lo, b_r)
    in_specs = [
        pl.BlockSpec((1, d), const),
        pl.BlockSpec((1, 1, d), lambda bi, i: (bi, 0, chunk0)),
        pl.BlockSpec((1, 1, d), lambda bi, i: (bi, 0, chunk0 + 1)),
        pl.BlockSpec((d, LANES), const),
        pl.BlockSpec((d, LANES), const),
        pl.BlockSpec((1, LANES), const),
    ]
    out_shapes = (jax.ShapeDtypeStruct((t * ROW_WORDS, LANES), jnp.uint32),
                  jax.ShapeDtypeStruct((t, TOP_K), jnp.int32),
                  jax.ShapeDtypeStruct((t, TOP_K), F32))
    out_specs = (pl.BlockSpec((tm * ROW_WORDS, LANES), flat),
                 pl.BlockSpec((tm, TOP_K), flat),
                 pl.BlockSpec((tm, TOP_K), flat))
    return inputs, in_specs, out_shapes, out_specs


def _issue_row_copies(n_rows, make_copy):
    def issue(it, carry):
        for u in range(DMA_UNROLL):
            make_copy(it * DMA_UNROLL + u).start(priority=u % 2)
        return carry

    lax.fori_loop(0, n_rows // DMA_UNROLL, issue, 0)


def _gather_kernel(nu_ref, tok_ref, tokn_ref, hp_ref, xs_ref, buf_ref, sem):
    i = pl.program_id(0)
    n_used = nu_ref[0]
    slot = lax.rem(i, 2)

    def issue_from(t_ref, dst_slot):
        def make_copy(r):
            src = pl.multiple_of(t_ref[0, 0, r] * ROW_WORDS, ROW_WORDS)
            return pltpu.make_async_copy(hp_ref.at[pl.ds(src, ROW_WORDS)],
                                         buf_ref.at[dst_slot, pl.ds(r * ROW_WORDS, ROW_WORDS)],
                                         sem.at[dst_slot])
        _issue_row_copies(GATHER_ROWS, make_copy)

    @pl.when(jnp.logical_and(i == 0, n_used > 0))
    def _():
        issue_from(tok_ref, 0)

    @pl.when(i + 1 < n_used)
    def _():
        issue_from(tokn_ref, 1 - slot)

    @pl.when(i < n_used)
    def _():
        pltpu.make_async_copy(hp_ref.at[pl.ds(0, GATHER_ROWS * ROW_WORDS)], buf_ref.at[slot],
                              sem.at[slot]).wait()
        xs_ref[...] = _unpack_rows(buf_ref.at[slot], GATHER_ROWS)

    @pl.when(i >= n_used)
    def _():
        xs_ref[...] = jnp.zeros_like(xs_ref)


def _gather_rows(n_used, slot_tok, hp, d):
    n_slots = slot_tok.shape[0]
    n_steps = n_slots // GATHER_ROWS
    tok3 = slot_tok.reshape(n_steps, 1, GATHER_ROWS)
    return pl.pallas_call(
        _gather_kernel,
        out_shape=jax.ShapeDtypeStruct((n_slots, d), BF16),
        grid_spec=pltpu.PrefetchScalarGridSpec(
            num_scalar_prefetch=1,
            grid=(n_steps,),
            in_specs=[
                pl.BlockSpec((1, 1, GATHER_ROWS), lambda i, nu: (i, 0, 0), memory_space=pltpu.SMEM),
                pl.BlockSpec((1, 1, GATHER_ROWS),
                             lambda i, nu: (jnp.minimum(i + 1, n_steps - 1), 0, 0),
                             memory_space=pltpu.SMEM),
                pl.BlockSpec(memory_space=pl.ANY),
            ],
            out_specs=pl.BlockSpec((GATHER_ROWS, d), lambda i, nu: (i, 0)),
            scratch_shapes=[pltpu.VMEM((2, GATHER_ROWS * ROW_WORDS, LANES), jnp.uint32),
                            pltpu.SemaphoreType.DMA((2,))],
        ),
        compiler_params=_cparams("arbitrary"),
        name="gather_rows",
    )(n_used, tok3, tok3, hp)


def _block_state(be_ref):
    bi = pl.program_id(1)
    used = bi < be_ref[pl.num_programs(1)]
    prev = be_ref[jnp.maximum(bi - 1, 0)]
    return used, jnp.logical_or(bi == 0, be_ref[bi] != prev)


def _gmm_gu_kernel(be_ref, x_ref, wg_ref, wl_ref, bg_ref, bl_ref, act_ref, wg_s, wl_s):
    used, changed = _block_state(be_ref)

    @pl.when(jnp.logical_and(used, changed))
    def _():
        wg_s[...] = wg_ref[...].astype(BF16)
        wl_s[...] = wl_ref[...].astype(BF16)

    @pl.when(used)
    def _():
        xb = x_ref[...]
        glu = jnp.dot(xb, wg_s[...], preferred_element_type=F32) + bg_ref[...]
        lin = jnp.dot(xb, wl_s[...], preferred_element_type=F32) + bl_ref[...]
        glu = jnp.minimum(glu, SWIGLU_LIMIT)
        lin = jnp.clip(lin, -SWIGLU_LIMIT, SWIGLU_LIMIT)
        sig = 1.0 / (1.0 + jnp.exp(-SWIGLU_ALPHA * glu))
        act_ref[...] = (glu * sig * (lin + 1.0)).astype(BF16)

    @pl.when(jnp.logical_not(used))
    def _():
        act_ref[...] = jnp.zeros_like(act_ref)


def _gmm_gu(block_e, xs, w_gu, b_gu4, layer):
    n_slots, d = xs.shape
    f = w_gu.shape[3] // 2
    tm, tn = EXPERT_TM, 1024
    nn = f // tn
    return pl.pallas_call(
        _gmm_gu_kernel,
        out_shape=jax.ShapeDtypeStruct((n_slots, f), BF16),
        grid_spec=pltpu.PrefetchScalarGridSpec(
            num_scalar_prefetch=1,
            grid=(nn, n_slots // tm),
            in_specs=[
                pl.BlockSpec((tm, d), lambda j, i, be: (i, 0)),
                pl.BlockSpec((None, None, d, tn), lambda j, i, be: (layer, be[i], 0, j)),
                pl.BlockSpec((None, None, d, tn), lambda j, i, be: (layer, be[i], 0, nn + j)),
                pl.BlockSpec((None, None, 1, tn), lambda j, i, be: (layer, be[i], 0, j)),
                pl.BlockSpec((None, None, 1, tn), lambda j, i, be: (layer, be[i], 0, nn + j)),
            ],
            out_specs=pl.BlockSpec((tm, tn), lambda j, i, be: (i, j)),
            scratch_shapes=[pltpu.VMEM((d, tn), BF16), pltpu.VMEM((d, tn), BF16)],
        ),
        compiler_params=_cparams("arbitrary", "arbitrary"),
        name="gmm_gate_up",
    )(block_e, xs, w_gu, w_gu, b_gu4, b_gu4)


DOWN_TN = SUBLANES * LANES


def _gmm_down_kernel(be_ref, a_ref, w_ref, b_ref, y_ref, w_s):
    used, changed = _block_state(be_ref)

    @pl.when(jnp.logical_and(used, changed))
    def _():
        w_s[...] = w_ref[...].astype(BF16)

    @pl.when(used)
    def _():
        y = jnp.dot(a_ref[...], w_s[...], preferred_element_type=F32) + b_ref[...]
        tm = y.shape[0]
        y2_ref = y_ref.reshape(tm * SUBLANES, LANES)
        for cc in range(SUBLANES):
            y2_ref[pl.ds(cc, tm, stride=SUBLANES), :] = y[:, cc * LANES:(cc + 1) * LANES]

    @pl.when(jnp.logical_not(used))
    def _():
        y_ref[...] = jnp.zeros_like(y_ref)


def _gmm_down(block_e, act, w_down, b_down4, layer):
    n_slots, f = act.shape
    d = w_down.shape[3]
    tm, tn = EXPERT_TM, DOWN_TN
    return pl.pallas_call(
        _gmm_down_kernel,
        out_shape=jax.ShapeDtypeStruct((n_slots, d // LANES, LANES), F32),
        grid_spec=pltpu.PrefetchScalarGridSpec(
            num_scalar_prefetch=1,
            grid=(d // tn, n_slots // tm),
            in_specs=[
                pl.BlockSpec((tm, f), lambda j, i, be: (i, 0)),
                pl.BlockSpec((None, None, f, tn), lambda j, i, be: (layer, be[i], 0, j)),
                pl.BlockSpec((None, None, 1, tn), lambda j, i, be: (layer, be[i], 0, j)),
            ],
            out_specs=pl.BlockSpec((tm, SUBLANES, LANES), lambda j, i, be: (i, j, 0)),
            scratch_shapes=[pltpu.VMEM((f, tn), BF16)],
        ),
        compiler_params=_cparams("arbitrary", "arbitrary"),
        name="gmm_down",
    )(block_e, act, w_down, b_down4)


COMBINE_ROWS = COMBINE_TOK * TOP_K
COMBINE_SUB = 32


def _combine_kernel(pos_ref, posn_ref, ys_ref, gate_ref, x_ref, g2_ref, gf_ref, out_ref,
                    buf_ref, sem, *, final_norm):
    i = pl.program_id(0)
    n = pl.num_programs(0)
    slot = lax.rem(i, 2)
    n_chunks = x_ref.shape[1] // LANES

    def issue_from(p_ref, dst_slot):
        def make_copy(r):
            src = pl.multiple_of(p_ref[0, 0, r] * n_chunks, n_chunks)
            return pltpu.make_async_copy(ys_ref.at[pl.ds(src, n_chunks)],
                                         buf_ref.at[dst_slot, pl.ds(r * n_chunks, n_chunks)],
                                         sem.at[dst_slot])
        _issue_row_copies(COMBINE_ROWS, make_copy)

    @pl.when(i == 0)
    def _():
        issue_from(pos_ref, 0)

    @pl.when(i + 1 < n)
    def _():
        issue_from(posn_ref, 1 - slot)

    pltpu.make_async_copy(ys_ref.at[pl.ds(0, COMBINE_ROWS * n_chunks)], buf_ref.at[slot],
                          sem.at[slot]).wait()

    rows_ref = buf_ref.at[slot]
    d = n_chunks * LANES
    for t0 in range(0, COMBINE_TOK, COMBINE_SUB):
        tok = slice(t0, t0 + COMBINE_SUB)
        gates = [jnp.broadcast_to(gate_ref[tok, k:k + 1], (COMBINE_SUB, LANES)) for k in range(TOP_K)]
        ssq = jnp.zeros((COMBINE_SUB, LANES), F32)
        for c in range(n_chunks):
            cols = slice(c * LANES, (c + 1) * LANES)
            y = None
            for k in range(TOP_K):
                first = (k * COMBINE_TOK + t0) * n_chunks + c
                rows = rows_ref[pl.ds(first, COMBINE_SUB, stride=n_chunks), :]
                y = gates[k] * rows if y is None else y + gates[k] * rows
            o = x_ref[tok, cols] + g2_ref[0, :, cols] * y
            out_ref[tok, cols] = o
            ssq = ssq + o * o
        if final_norm:
            inv = lax.rsqrt(jnp.sum(ssq, axis=-1, keepdims=True) / d + RMS_EPS)
            out_ref[tok, :] = out_ref[tok, :] * inv * gf_ref[...]


def _combine(pos, ys2, gate, x2, mod3, gate_chunk, tokens_per_batch, final_g, final_norm):
    t, d = x2.shape
    n_chunks = d // LANES
    n_steps = t // COMBINE_TOK
    steps_per_batch = tokens_per_batch // COMBINE_TOK
    pos3 = pos.reshape(n_steps, COMBINE_TOK, TOP_K).transpose(0, 2, 1).reshape(n_steps, 1, COMBINE_ROWS)
    return pl.pallas_call(
        functools.partial(_combine_kernel, final_norm=final_norm),
        out_shape=jax.ShapeDtypeStruct((t, d), F32),
        grid=(n_steps,),
        in_specs=[
            pl.BlockSpec((1, 1, COMBINE_ROWS), lambda i: (i, 0, 0), memory_space=pltpu.SMEM),
            pl.BlockSpec((1, 1, COMBINE_ROWS), lambda i: (jnp.minimum(i + 1, n_steps - 1), 0, 0),
                         memory_space=pltpu.SMEM),
            pl.BlockSpec(memory_space=pl.ANY),
            pl.BlockSpec((COMBINE_TOK, TOP_K), lambda i: (i, 0)),
            pl.BlockSpec((COMBINE_TOK, d), lambda i: (i, 0)),
            pl.BlockSpec((1, 1, d), lambda i: (i // steps_per_batch, 0, gate_chunk)),
            pl.BlockSpec((1, d), lambda i: (0, 0)),
        ],
        out_specs=pl.BlockSpec((COMBINE_TOK, d), lambda i: (i, 0)),
        scratch_shapes=[pltpu.VMEM((2, COMBINE_ROWS * n_chunks, LANES), F32),
                        pltpu.SemaphoreType.DMA((2,))],
        compiler_params=_cparams("arbitrary"),
        name="combine",
    )(pos3, pos3, ys2, gate, x2, mod3, final_g)


SCATTER_CHUNK = 8192
TOP_K_SHIFT = 2
assert 1 << TOP_K_SHIFT == TOP_K and DMA_UNROLL % TOP_K == 0


def _slot_tok_kernel(pos_ref, zeros_ref, out_ref):
    i = pl.program_id(0)

    @pl.when(i == 0)
    def _():
        pltpu.sync_copy(zeros_ref, out_ref)

    base = i * SCATTER_CHUNK

    def scatter(it, carry):
        a0 = it * DMA_UNROLL
        tok0 = lax.shift_right_logical(base + a0, TOP_K_SHIFT)
        for u in range(DMA_UNROLL):
            out_ref[pos_ref[0, 0, a0 + u]] = tok0 + (u >> TOP_K_SHIFT)
        return carry

    lax.fori_loop(0, SCATTER_CHUNK // DMA_UNROLL, scatter, 0)


def _slot_tokens(pos, n_slots):
    n_steps = pos.shape[0] // SCATTER_CHUNK
    return pl.pallas_call(
        _slot_tok_kernel,
        out_shape=jax.ShapeDtypeStruct((n_slots,), jnp.int32),
        grid=(n_steps,),
        in_specs=[pl.BlockSpec((1, 1, SCATTER_CHUNK), lambda i: (i, 0, 0),
                               memory_space=pltpu.SMEM),
                  pl.BlockSpec(memory_space=pl.ANY)],
        out_specs=pl.BlockSpec(memory_space=pltpu.SMEM),
        compiler_params=_cparams("arbitrary"),
        name="slot_tokens",
    )(pos.reshape(n_steps, 1, SCATTER_CHUNK), jnp.zeros((n_slots,), jnp.int32))


def _slot_plan(top_idx):
    t = top_idx.shape[0]
    n_assign = t * TOP_K
    flat_e = top_idx.reshape(-1)
    onehot = (flat_e[:, None] == jnp.arange(N_EXPERTS, dtype=jnp.int32)[None, :]).astype(jnp.int32)
    csum = jnp.cumsum(onehot, axis=0)
    counts = csum[-1]
    rank = jnp.sum(csum * onehot, axis=1) - 1
    padded = (counts + EXPERT_TM - 1) // EXPERT_TM * EXPERT_TM
    padded_end = jnp.cumsum(padded)
    padded_start = padded_end - padded
    pos = (padded_start[flat_e] + rank).astype(jnp.int32)
    n_blocks = -(-(n_assign + N_EXPERTS * (EXPERT_TM - 1)) // EXPERT_TM)
    n_blocks = -(-n_blocks * EXPERT_TM // GATHER_ROWS) * GATHER_ROWS // EXPERT_TM
    n_slots = n_blocks * EXPERT_TM
    slot_tok = _slot_tokens(pos, n_slots)
    block_start = jnp.arange(n_blocks, dtype=jnp.int32) * EXPERT_TM
    block_e = jnp.minimum(jnp.sum(block_start[:, None] >= padded_end[None, :], axis=1),
                          N_EXPERTS - 1).astype(jnp.int32)
    n_used = (padded_end[-1:] // EXPERT_TM).astype(jnp.int32)
    return pos, slot_tok, jnp.concatenate([block_e, n_used]), n_used


def _expert_ffn_residual(x, routed, mod3, layer, w_gu, b_gu4, w_down, b_down4, final_g,
                         final_norm):
    b, l, d = x.shape
    assert d == 2 * ROW_WORDS * LANES
    hp, top_idx, gate = routed
    pos, slot_tok, block_e, n_used = _slot_plan(top_idx)
    xs = _gather_rows(n_used, slot_tok, hp, d)
    act = _gmm_gu(block_e, xs, w_gu, b_gu4, layer)
    ys3 = _gmm_down(block_e, act, w_down, b_down4, layer)
    ys2 = ys3.reshape(ys3.shape[0] * ys3.shape[1], LANES)
    out = _combine(pos, ys2, gate, x.reshape(b * l, d), mod3, 5, l, final_g, final_norm)
    return out.reshape(b, l, d)


def _rope_tables(length):
    rows = length // GRID_W
    row = jnp.repeat(jnp.arange(rows), GRID_W).astype(F32)
    col = jnp.tile(jnp.arange(GRID_W), rows).astype(F32)
    inv = 1.0 / (ROPE_THETA ** (jnp.arange(0, ROPE_AXIS, 2, dtype=F32) / ROPE_AXIS))
    ang_r = row[:, None] * inv[None, :]
    ang_c = col[:, None] * inv[None, :]
    cos = jnp.concatenate([jnp.cos(ang_r)] * 2 + [jnp.cos(ang_c)] * 2, axis=-1)
    sin = jnp.concatenate([jnp.sin(ang_r)] * 2 + [jnp.sin(ang_c)] * 2, axis=-1)
    return cos, sin


def _rotate_half_cols(w):
    a, b_, c_, d_ = jnp.split(w, 4, axis=-1)
    return jnp.concatenate([-b_, a, -d_, c_], axis=-1)


def kernel(x, c, ctx, c_ctx, ada_w, ada_b, norm_mix_g, norm_ffn_g, mla_w_in, mla_q_norm_g,
           mla_kv_norm_g, mla_w_q_up, mla_w_kv_up, mla_w_out, conv_w_in, conv_w, conv_w_out,
           router_w, router_b, expert_w_gu, expert_b_gu, expert_w_down, expert_b_down,
           final_norm_g):
    b, l, d = x.shape
    depth = ada_w.shape[0]
    lc = ctx.shape[1]
    assert depth == 2 and b + 1 <= ADA_ROWS

    cv = jnp.concatenate([c, c_ctx[None, :], jnp.zeros((ADA_ROWS - b - 1, d), F32)], axis=0)
    ctx_row = b
    ada_b3 = ada_b.reshape(depth, 1, N_ADA * d)
    cos, sin = _rope_tables(l)
    b_gu4 = expert_b_gu.reshape(depth, N_EXPERTS, 1, -1)
    b_down4 = expert_b_down.reshape(depth, N_EXPERTS, 1, d)
    w_r = jnp.pad(router_w, ((0, 0), (0, 0), (0, LANES - N_EXPERTS)))
    b_r = jnp.pad(router_b, ((0, 0), (0, LANES - N_EXPERTS)))[:, None, :]
    final_g = final_norm_g[None, :]

    mod3 = _ada(cv, ada_w, ada_b3, 0).reshape(ADA_ROWS, 1, N_ADA * d)
    w_in = mla_w_in[0]
    w_kr = w_in[:, Q_LORA + KV_LORA:]
    w_all = jnp.concatenate([w_in, _rotate_half_cols(w_kr)], axis=1).astype(BF16)
    w_ctx = w_in[:, Q_LORA:].astype(BF16)
    gq = mla_q_norm_g[0][None, :]
    gkv = mla_kv_norm_g[0][None, :]
    g_mix = norm_mix_g[0][None, :]
    wq = mla_w_q_up[0].reshape(Q_LORA, N_HEADS, QK_DIM)
    wq_rope = wq[..., QK_NOPE:]
    wq_all = jnp.concatenate([wq, _rotate_half_cols(wq_rope)], axis=-1)
    wq_all = wq_all.reshape(Q_LORA, N_HEADS * Q_HEAD_COLS).astype(BF16)
    wkv = mla_w_kv_up[0].astype(BF16)
    w_o = mla_w_out[0].astype(BF16)

    cq, ckv, kr = _mla_in(x, g_mix, mod3, 0, w_all, gq, gkv, cos, sin)
    ckv_c, kr_c = _mla_in_ctx(ctx, g_mix, mod3, ctx_row, w_ctx, gkv)
    ckv_all = jnp.concatenate([ckv_c, ckv], axis=1)
    kr_all = jnp.concatenate([kr_c, kr], axis=1)
    q = _q_up(cq, wq_all, cos, sin)
    k, v = _kv_up(ckv_all, kr_all, wkv, 768)
    o = _attention(q, k, v, 1024, 768)
    x, routed = _proj_res(o, w_o, x, mod3, 2, norm_ffn_g[0][None, :], w_r[0], b_r[0])
    x = _expert_ffn_residual(x, routed, mod3, 0, expert_w_gu, b_gu4, expert_w_down, b_down4,
                             final_g, False)

    mod3 = _ada(cv, ada_w, ada_b3, 1).reshape(ADA_ROWS, 1, N_ADA * d)
    gb, z = _conv_in(x, norm_mix_g[1][None, :], mod3, 0, conv_w_in[0].astype(BF16))
    x, routed = _conv_out(z, gb, conv_w[0], conv_w_out[0].astype(BF16), x, mod3, 2,
                          norm_ffn_g[1][None, :], w_r[1], b_r[1])
    x = _expert_ffn_residual(x, routed, mod3, 1, expert_w_gu, b_gu4, expert_w_down, b_down4,
                             final_g, True)
    return x
```

```python
import functools

import jax
import jax.numpy as jnp
import numpy as np
from jax import lax
from jax.experimental import pallas as pl
from jax.experimental.pallas import tpu as pltpu

GRID_W = 64
RMS_EPS = 1e-6
N_ADA = 6
N_HEADS = 16
Q_LORA = 512
KV_LORA = 512
QK_NOPE = 128
QK_ROPE = 64
V_DIM = 128
ROPE_AXIS = QK_ROPE // 2
ROPE_THETA = 10000.0
QK_DIM = QK_NOPE + QK_ROPE
ATTN_SCALE = QK_DIM ** -0.5
LOG2_E = 1.4426950408889634
Q_SCALE = ATTN_SCALE * LOG2_E
N_EXPERTS = 32
TOP_K = 4
SWIGLU_LIMIT = 7.0
SWIGLU_ALPHA = 1.702

LANES = 128
SUBLANES = 8
VMEM_LIMIT = 56 * 1024 * 1024

ADA_ROWS = 8
EXPERT_TM = 512
GATHER_ROWS = EXPERT_TM
COMBINE_TOK = 128
ROW_WORDS = SUBLANES
DMA_UNROLL = 8

BF16 = jnp.bfloat16
F32 = jnp.float32


def _cparams(*sem):
    return pltpu.CompilerParams(dimension_semantics=sem, vmem_limit_bytes=VMEM_LIMIT)


def _rms(x):
    return x * lax.rsqrt(jnp.mean(x * x, axis=-1, keepdims=True) + RMS_EPS)


def _modulate(x, g, shift, scale):
    return _rms(x) * g * (1.0 + scale) + shift


def _ada_kernel(cv_ref, w_ref, b_ref, o_ref):
    cv = cv_ref[...]
    s = cv * (1.0 / (1.0 + jnp.exp(-cv)))
    o_ref[...] = jnp.dot(s, w_ref[...], preferred_element_type=F32,
                         precision=lax.Precision.HIGHEST) + b_ref[...]


def _ada(cv, ada_w, ada_b3, layer):
    d = cv.shape[1]
    n = ada_w.shape[2]
    tn = 1024
    return pl.pallas_call(
        _ada_kernel,
        out_shape=jax.ShapeDtypeStruct((ADA_ROWS, n), F32),
        grid=(n // tn,),
        in_specs=[
            pl.BlockSpec((ADA_ROWS, d), lambda j: (0, 0)),
            pl.BlockSpec((None, d, tn), lambda j: (layer, 0, j)),
            pl.BlockSpec((None, 1, tn), lambda j: (layer, 0, j)),
        ],
        out_specs=pl.BlockSpec((ADA_ROWS, tn), lambda j: (0, j)),
        compiler_params=_cparams("arbitrary"),
        name="ada",
    )(cv, ada_w, ada_b3)


def _mla_in_kernel(x_ref, g_ref, sh_ref, sc_ref, w_ref, gq_ref, gkv_ref, cos_ref, sin_ref,
                   cq_ref, ckv_ref, kr_ref):
    h = _modulate(x_ref[0], g_ref[...], sh_ref[0], sc_ref[0]).astype(BF16)
    p = jnp.dot(h, w_ref[...], preferred_element_type=F32)
    cq_ref[0] = (_rms(p[:, :Q_LORA]) * gq_ref[...]).astype(BF16)
    ckv_ref[0] = (_rms(p[:, Q_LORA:Q_LORA + KV_LORA]) * gkv_ref[...]).astype(BF16)
    o = Q_LORA + KV_LORA
    kr = p[:, o:o + QK_ROPE]
    kr_rot = p[:, o + QK_ROPE:o + 2 * QK_ROPE]
    kr_ref[0] = (kr * cos_ref[...] + kr_rot * sin_ref[...]).astype(BF16)


def _mla_in_ctx_kernel(x_ref, g_ref, sh_ref, sc_ref, w_ref, gkv_ref, ckv_ref, kr_ref):
    h = _modulate(x_ref[0], g_ref[...], sh_ref[0], sc_ref[0]).astype(BF16)
    p = jnp.dot(h, w_ref[...], preferred_element_type=F32)
    ckv_ref[0] = (_rms(p[:, :KV_LORA]) * gkv_ref[...]).astype(BF16)
    kr_ref[0] = p[:, KV_LORA:KV_LORA + QK_ROPE].astype(BF16)


def _mla_in(x, g, mod3, layer_chunk0, w_all, gq, gkv, cos, sin):
    b, l, d = x.shape
    tm = 512
    nw = w_all.shape[1]
    row = lambda bi, i: (bi, i, 0)
    return pl.pallas_call(
        _mla_in_kernel,
        out_shape=(jax.ShapeDtypeStruct((b, l, Q_LORA), BF16),
                   jax.ShapeDtypeStruct((b, l, KV_LORA), BF16),
                   jax.ShapeDtypeStruct((b, l, QK_ROPE), BF16)),
        grid=(b, l // tm),
        in_specs=[
            pl.BlockSpec((1, tm, d), row),
            pl.BlockSpec((1, d), lambda bi, i: (0, 0)),
            pl.BlockSpec((1, 1, d), lambda bi, i: (bi, 0, layer_chunk0)),
            pl.BlockSpec((1, 1, d), lambda bi, i: (bi, 0, layer_chunk0 + 1)),
            pl.BlockSpec((d, nw), lambda bi, i: (0, 0)),
            pl.BlockSpec((1, Q_LORA), lambda bi, i: (0, 0)),
            pl.BlockSpec((1, KV_LORA), lambda bi, i: (0, 0)),
            pl.BlockSpec((tm, QK_ROPE), lambda bi, i: (i, 0)),
            pl.BlockSpec((tm, QK_ROPE), lambda bi, i: (i, 0)),
        ],
        out_specs=(pl.BlockSpec((1, tm, Q_LORA), row),
                   pl.BlockSpec((1, tm, KV_LORA), row),
                   pl.BlockSpec((1, tm, QK_ROPE), row)),
        compiler_params=_cparams("arbitrary", "arbitrary"),
        name="mla_in",
    )(x, g, mod3, mod3, w_all, gq, gkv, cos, sin)


def _mla_in_ctx(ctx, g, mod3, ctx_row, w_kv, gkv):
    b, l, d = ctx.shape
    tm = l
    nw = w_kv.shape[1]
    row = lambda bi, i: (bi, i, 0)
    return pl.pallas_call(
        _mla_in_ctx_kernel,
        out_shape=(jax.ShapeDtypeStruct((b, l, KV_LORA), BF16),
                   jax.ShapeDtypeStruct((b, l, QK_ROPE), BF16)),
        grid=(b, l // tm),
        in_specs=[
            pl.BlockSpec((1, tm, d), row),
            pl.BlockSpec((1, d), lambda bi, i: (0, 0)),
            pl.BlockSpec((1, 1, d), lambda bi, i: (ctx_row, 0, 0)),
            pl.BlockSpec((1, 1, d), lambda bi, i: (ctx_row, 0, 1)),
            pl.BlockSpec((d, nw), lambda bi, i: (0, 0)),
            pl.BlockSpec((1, KV_LORA), lambda bi, i: (0, 0)),
        ],
        out_specs=(pl.BlockSpec((1, tm, KV_LORA), row),
                   pl.BlockSpec((1, tm, QK_ROPE), row)),
        compiler_params=_cparams("arbitrary", "arbitrary"),
        name="mla_in_ctx",
    )(ctx, g, mod3, mod3, w_kv, gkv)


Q_HEAD_COLS = QK_NOPE + 2 * QK_ROPE
KV_HEAD_COLS = QK_NOPE + V_DIM


def _q_up_kernel(cq_ref, w_ref, cos_ref, sin_ref, q_ref):
    cq = cq_ref[0]
    cos = cos_ref[...]
    sin = sin_ref[...]
    for h in range(N_HEADS):
        p = jnp.dot(cq, w_ref[:, h * Q_HEAD_COLS:(h + 1) * Q_HEAD_COLS],
                    preferred_element_type=F32)
        rope = p[:, QK_NOPE:QK_NOPE + QK_ROPE] * cos + p[:, QK_NOPE + QK_ROPE:] * sin
        q_ref[0, h, :, :QK_NOPE] = (p[:, :QK_NOPE] * Q_SCALE).astype(BF16)
        q_ref[0, h, :, QK_NOPE:] = (rope * Q_SCALE).astype(BF16)


def _q_up(cq, w_q, cos, sin):
    b, l, _ = cq.shape
    tm = 512
    return pl.pallas_call(
        _q_up_kernel,
        out_shape=jax.ShapeDtypeStruct((b, N_HEADS, l, QK_DIM), BF16),
        grid=(b, l // tm),
        in_specs=[
            pl.BlockSpec((1, tm, Q_LORA), lambda bi, i: (bi, i, 0)),
            pl.BlockSpec(w_q.shape, lambda bi, i: (0, 0)),
            pl.BlockSpec((tm, QK_ROPE), lambda bi, i: (i, 0)),
            pl.BlockSpec((tm, QK_ROPE), lambda bi, i: (i, 0)),
        ],
        out_specs=pl.BlockSpec((1, N_HEADS, tm, QK_DIM), lambda bi, i: (bi, 0, i, 0)),
        compiler_params=_cparams("arbitrary", "arbitrary"),
        name="q_up",
    )(cq, w_q, cos, sin)


def _kv_up_kernel(ckv_ref, kr_ref, w_ref, k_ref, v_ref):
    ckv = ckv_ref[0]
    kr = kr_ref[0]
    for h in range(N_HEADS):
        p = jnp.dot(ckv, w_ref[:, h * KV_HEAD_COLS:(h + 1) * KV_HEAD_COLS],
                    preferred_element_type=F32)
        k_ref[0, h, :, :QK_NOPE] = p[:, :QK_NOPE].astype(BF16)
        k_ref[0, h, :, QK_NOPE:] = kr
        v_ref[0, h] = p[:, QK_NOPE:].astype(BF16)


def _kv_up(ckv, kr, w_kv, tm):
    b, l, _ = ckv.shape
    return pl.pallas_call(
        _kv_up_kernel,
        out_shape=(jax.ShapeDtypeStruct((b, N_HEADS, l, QK_DIM), BF16),
                   jax.ShapeDtypeStruct((b, N_HEADS, l, V_DIM), BF16)),
        grid=(b, l // tm),
        in_specs=[
            pl.BlockSpec((1, tm, KV_LORA), lambda bi, i: (bi, i, 0)),
            pl.BlockSpec((1, tm, QK_ROPE), lambda bi, i: (bi, i, 0)),
            pl.BlockSpec(w_kv.shape, lambda bi, i: (0, 0)),
        ],
        out_specs=(pl.BlockSpec((1, N_HEADS, tm, QK_DIM), lambda bi, i: (bi, 0, i, 0)),
                   pl.BlockSpec((1, N_HEADS, tm, V_DIM), lambda bi, i: (bi, 0, i, 0))),
        compiler_params=_cparams("arbitrary", "arbitrary"),
        name="kv_up",
    )(ckv, kr, w_kv)


def _attn_kernel(q_ref, k_ref, v_ref, o_ref, *, tk, n_chunks):
    q = q_ref[0, 0]
    tq = q.shape[0]
    n_slabs = tk // LANES
    m = jnp.full((tq, LANES), -jnp.inf, F32)
    l_loc = jnp.zeros((tq, LANES), F32)
    acc = jnp.zeros((tq, V_DIM), F32)
    for c in range(n_chunks):
        k = k_ref[0, 0, c * tk:(c + 1) * tk, :]
        s = lax.dot_general(q, k, (((1,), (1,)), ((), ())), preferred_element_type=F32)
        slabs = [s[:, j * LANES:(j + 1) * LANES] for j in range(n_slabs)]
        m_loc = slabs[0]
        for slab in slabs[1:]:
            m_loc = jnp.maximum(m_loc, slab)
        m_new = jnp.maximum(m, jnp.max(m_loc, axis=-1, keepdims=True))
        alpha = jnp.exp2(m - m_new)
        ps = [jnp.exp2(slab - m_new) for slab in slabs]
        p_sum = ps[0]
        for p in ps[1:]:
            p_sum = p_sum + p
        l_loc = alpha * l_loc + p_sum
        p_bf = jnp.concatenate([p.astype(BF16) for p in ps], axis=-1)
        acc = alpha * acc + jnp.dot(p_bf, v_ref[0, 0, c * tk:(c + 1) * tk, :],
                                    preferred_element_type=F32)
        m = m_new
    o_ref[0] = (acc / jnp.sum(l_loc, axis=-1, keepdims=True)).astype(BF16)


def _attention(q, k, v, tq, tk):
    b, h, l, _ = q.shape
    lk = k.shape[2]
    return pl.pallas_call(
        functools.partial(_attn_kernel, tk=tk, n_chunks=lk // tk),
        out_shape=jax.ShapeDtypeStruct((b, l, h * V_DIM), BF16),
        grid=(b, h, l // tq),
        in_specs=[
            pl.BlockSpec((1, 1, tq, QK_DIM), lambda bi, hi, i: (bi, hi, i, 0)),
            pl.BlockSpec((1, 1, lk, QK_DIM), lambda bi, hi, i: (bi, hi, 0, 0)),
            pl.BlockSpec((1, 1, lk, V_DIM), lambda bi, hi, i: (bi, hi, 0, 0)),
        ],
        out_specs=pl.BlockSpec((1, tq, V_DIM), lambda bi, hi, i: (bi, i, hi)),
        compiler_params=_cparams("arbitrary", "arbitrary", "arbitrary"),
        name="attention",
    )(q, k, v)


def _proj_res_kernel(o_ref, w_ref, x_ref, g1_ref, *refs):
    route_in, (out_ref, *route_out) = refs[:6], refs[6:]
    mix = jnp.dot(o_ref[0], w_ref[...], preferred_element_type=F32)
    x1 = x_ref[0] + g1_ref[0] * mix
    out_ref[0] = x1
    _route_tile(x1, *route_in, *route_out)


def _proj_res(o, w_out, x, mod3, gate_chunk, norm_ffn_g, w_r, b_r):
    b, l, d = x.shape
    tm = 512
    row = lambda bi, i: (bi, i, 0)
    r_in, r_specs, r_shapes, r_out_specs = _route_specs(b, l, d, tm, norm_ffn_g, mod3, 3, w_r, b_r)
    x1, *routed = pl.pallas_call(
        _proj_res_kernel,
        out_shape=(jax.ShapeDtypeStruct((b, l, d), F32), *r_shapes),
        grid=(b, l // tm),
        in_specs=[
            pl.BlockSpec((1, tm, o.shape[2]), row),
            pl.BlockSpec(w_out.shape, lambda bi, i: (0, 0)),
            pl.BlockSpec((1, tm, d), row),
            pl.BlockSpec((1, 1, d), lambda bi, i: (bi, 0, gate_chunk)),
            *r_specs,
        ],
        out_specs=(pl.BlockSpec((1, tm, d), row), *r_out_specs),
        compiler_params=_cparams("arbitrary", "arbitrary"),
        name="proj_res",
    )(o, w_out, x, mod3, *r_in)
    return x1, routed


def _conv_in_kernel(x_ref, g_ref, sh_ref, sc_ref, wb_ref, wc_ref, wu_ref, gb_ref, z_ref, h_ref):
    @pl.when(pl.program_id(2) == 0)
    def _():
        h_ref[...] = _modulate(x_ref[0], g_ref[...], sh_ref[0], sc_ref[0]).astype(BF16)

    h = h_ref[...]
    gb_ref[0] = jnp.dot(h, wb_ref[...], preferred_element_type=F32).astype(BF16)
    gc = jnp.dot(h, wc_ref[...], preferred_element_type=F32)
    u = jnp.dot(h, wu_ref[...], preferred_element_type=F32)
    z_ref[0] = (gc * u).astype(BF16)


def _conv_in(x, g, mod3, chunk0, w_in):
    b, l, d = x.shape
    tm, tn = 512, 1024
    nn = d // tn
    row = lambda bi, i, j: (bi, i, 0)
    col = lambda bi, i, j: (bi, i, j)
    return pl.pallas_call(
        _conv_in_kernel,
        out_shape=(jax.ShapeDtypeStruct((b, l, d), BF16), jax.ShapeDtypeStruct((b, l, d), BF16)),
        grid=(b, l // tm, nn),
        in_specs=[
            pl.BlockSpec((1, tm, d), row),
            pl.BlockSpec((1, d), lambda bi, i, j: (0, 0)),
            pl.BlockSpec((1, 1, d), lambda bi, i, j: (bi, 0, chunk0)),
            pl.BlockSpec((1, 1, d), lambda bi, i, j: (bi, 0, chunk0 + 1)),
            pl.BlockSpec((d, tn), lambda bi, i, j: (0, j)),
            pl.BlockSpec((d, tn), lambda bi, i, j: (0, nn + j)),
            pl.BlockSpec((d, tn), lambda bi, i, j: (0, 2 * nn + j)),
        ],
        out_specs=(pl.BlockSpec((1, tm, tn), col), pl.BlockSpec((1, tm, tn), col)),
        scratch_shapes=[pltpu.VMEM((tm, d), BF16)],
        compiler_params=_cparams("arbitrary", "arbitrary", "arbitrary"),
        name="conv_in",
    )(x, g, mod3, mod3, w_in, w_in, w_in)


def _conv_out_kernel(z_ref, zp_ref, zn_ref, gb_ref, cw_ref, w_ref, x_ref, g1_ref, *refs):
    route_in, (out_ref, *route_out) = refs[:6], refs[6:]
    i = pl.program_id(1)
    last = pl.num_programs(1) - 1
    z = z_ref[0].astype(F32)
    tm = z.shape[0]
    prev_row = jnp.where(i > 0, zp_ref[0, SUBLANES - 1:SUBLANES, :].astype(F32), 0.0)
    next_row = jnp.where(i < last, zn_ref[0, 0:1, :].astype(F32), 0.0)
    ridx = lax.broadcasted_iota(jnp.int32, z.shape, 0)
    z_prev = jnp.where(ridx == 0, prev_row, pltpu.roll(z, 1, 0))
    z_next = jnp.where(ridx == tm - 1, next_row, pltpu.roll(z, tm - 1, 0))
    conv = cw_ref[0:1, :] * z_prev + cw_ref[1:2, :] * z + cw_ref[2:3, :] * z_next
    y = (gb_ref[0].astype(F32) * conv).astype(BF16)
    mix = jnp.dot(y, w_ref[...], preferred_element_type=F32)
    x1 = x_ref[0] + g1_ref[0] * mix
    out_ref[0] = x1
    _route_tile(x1, *route_in, *route_out)


def _conv_out(z, gb, conv_w, w_out, x, mod3, gate_chunk, norm_ffn_g, w_r, b_r):
    b, l, d = x.shape
    tm = 256
    hb = tm // SUBLANES
    n_halo = l // SUBLANES
    row = lambda bi, i: (bi, i, 0)
    r_in, r_specs, r_shapes, r_out_specs = _route_specs(b, l, d, tm, norm_ffn_g, mod3, 3, w_r, b_r)
    x1, *routed = pl.pallas_call(
        _conv_out_kernel,
        out_shape=(jax.ShapeDtypeStruct((b, l, d), F32), *r_shapes),
        grid=(b, l // tm),
        in_specs=[
            pl.BlockSpec((1, tm, d), row),
            pl.BlockSpec((1, SUBLANES, d), lambda bi, i: (bi, jnp.maximum(i * hb - 1, 0), 0)),
            pl.BlockSpec((1, SUBLANES, d),
                         lambda bi, i: (bi, jnp.minimum((i + 1) * hb, n_halo - 1), 0)),
            pl.BlockSpec((1, tm, d), row),
            pl.BlockSpec(conv_w.shape, lambda bi, i: (0, 0)),
            pl.BlockSpec(w_out.shape, lambda bi, i: (0, 0)),
            pl.BlockSpec((1, tm, d), row),
            pl.BlockSpec((1, 1, d), lambda bi, i: (bi, 0, gate_chunk)),
            *r_specs,
        ],
        out_specs=(pl.BlockSpec((1, tm, d), row), *r_out_specs),
        compiler_params=_cparams("arbitrary", "arbitrary"),
        name="conv_out",
    )(z, z, z, gb, conv_w, w_out, x, mod3, *r_in)
    return x1, routed


def _pack_rows(h, hp_ref):
    tm, d = h.shape
    bits = pltpu.bitcast(h.astype(BF16).astype(F32), jnp.uint32)
    words = (bits[:, :d // 2] >> 16) | (bits[:, d // 2:] & jnp.uint32(0xFFFF0000))
    for s in range(ROW_WORDS):
        hp_ref[pl.ds(s, tm, stride=ROW_WORDS), :] = words[:, s * LANES:(s + 1) * LANES]


def _unpack_rows(buf_ref, n_rows):
    lows, highs = [], []
    for s in range(ROW_WORDS):
        w = buf_ref[pl.ds(s, n_rows, stride=ROW_WORDS), :]
        lows.append(pltpu.bitcast(w << 16, F32).astype(BF16))
        highs.append(pltpu.bitcast(w & jnp.uint32(0xFFFF0000), F32).astype(BF16))
    return jnp.concatenate(lows + highs, axis=-1)


def _route_tile(x, g_ref, sh_ref, sc_ref, whi_ref, wlo_ref, br_ref, hp_ref, idx_ref, gate_ref,
                rank_ref, cnt_ref):
    h = _modulate(x, g_ref[...], sh_ref[0], sc_ref[0])
    _pack_rows(h, hp_ref)
    h_hi = h.astype(BF16)
    h_lo = (h - h_hi.astype(F32)).astype(BF16)
    logits = (jnp.dot(h_hi, whi_ref[...], preferred_element_type=F32)
              + (jnp.dot(h_lo, whi_ref[...], preferred_element_type=F32)
                 + jnp.dot(h_hi, wlo_ref[...], preferred_element_type=F32))) + br_ref[...]
    lane = lax.broadcasted_iota(jnp.int32, logits.shape, 1).astype(F32)
    work = jnp.where(lane < N_EXPERTS, logits, -jnp.inf)
    vals, idxs, picks = [], [], []
    for _ in range(TOP_K):
        mx = jnp.max(work, axis=-1, keepdims=True)
        ix = jnp.min(jnp.where(work == mx, lane, float(LANES)), axis=-1, keepdims=True)
        vals.append(mx)
        idxs.append(ix.astype(jnp.int32))
        picks.append(lane == ix)
        work = jnp.where(picks[-1], -jnp.inf, work)
    ex = [jnp.exp(v - vals[0]) for v in vals]
    den = ex[0] + ex[1] + ex[2] + ex[3]

    @pl.when(jnp.logical_and(pl.program_id(0) == 0, pl.program_id(1) == 0))
    def _():
        cnt_ref[...] = jnp.zeros_like(cnt_ref)

    tm = x.shape[0]
    picked = picks[0] | picks[1] | picks[2] | picks[3]
    tile_cnt = jnp.where(picked, 1.0, 0.0)
    earlier = (lax.broadcasted_iota(jnp.int32, (tm, tm), 1)
               < lax.broadcasted_iota(jnp.int32, (tm, tm), 0))
    before = jnp.dot(jnp.where(earlier, 1.0, 0.0).astype(BF16), tile_cnt.astype(BF16),
                     preferred_element_type=F32) + cnt_ref[...]
    cnt_ref[...] = cnt_ref[...] + jnp.sum(tile_cnt, axis=0, keepdims=True)
    for k in range(TOP_K):
        idx_ref[:, k:k + 1] = idxs[k]
        gate_ref[:, k:k + 1] = ex[k] / den
        rank = jnp.sum(jnp.where(picks[k], before, 0.0), axis=-1, keepdims=True)
        rank_ref[:, k:k + 1] = rank.astype(jnp.int32)


def _route_specs(b, l, d, tm, norm_g, mod3, chunk0, w_r, b_r):
    nt = l // tm
    t = b * l
    w_hi = w_r.astype(BF16)
    w_lo = (w_r - w_hi.astype(F32)).astype(BF16)
    flat = lambda bi, i: (bi * nt + i, 0)
    const = lambda bi, i: (0, 0)
    inputs = (norm_g, mod3, mod3, w_hi, w_lo, b_r)
    in_specs = [
        pl.BlockSpec((1, d), const),
        pl.BlockSpec((1, 1, d), lambda bi, i: (bi, 0, chunk0)),
        pl.BlockSpec((1, 1, d), lambda bi, i: (bi, 0, chunk0 + 1)),
        pl.BlockSpec((d, LANES), const),
        pl.BlockSpec((d, LANES), const),
        pl.BlockSpec((1, LANES), const),
    ]
    out_shapes = (jax.ShapeDtypeStruct((t * ROW_WORDS, LANES), jnp.uint32),
                  jax.ShapeDtypeStruct((t, TOP_K), jnp.int32),
                  jax.ShapeDtypeStruct((t, TOP_K), F32),
                  jax.ShapeDtypeStruct((t, TOP_K), jnp.int32),
                  jax.ShapeDtypeStruct((1, LANES), F32))
    out_specs = (pl.BlockSpec((tm * ROW_WORDS, LANES), flat),
                 pl.BlockSpec((tm, TOP_K), flat),
                 pl.BlockSpec((tm, TOP_K), flat),
                 pl.BlockSpec((tm, TOP_K), flat),
                 pl.BlockSpec((1, LANES), const))
    return inputs, in_specs, out_shapes, out_specs


def _issue_row_copies(n_rows, make_copy):
    def issue(it, carry):
        for u in range(DMA_UNROLL):
            make_copy(it * DMA_UNROLL + u).start(priority=u % 2)
        return carry

    lax.fori_loop(0, n_rows // DMA_UNROLL, issue, 0)


def _gather_kernel(nu_ref, tok_ref, tokn_ref, hp_ref, xs_ref, buf_ref, sem):
    i = pl.program_id(0)
    n_used = nu_ref[0]
    slot = lax.rem(i, 2)

    def issue_from(t_ref, dst_slot):
        def make_copy(r):
            src = pl.multiple_of(t_ref[0, 0, r] * ROW_WORDS, ROW_WORDS)
            return pltpu.make_async_copy(hp_ref.at[pl.ds(src, ROW_WORDS)],
                                         buf_ref.at[dst_slot, pl.ds(r * ROW_WORDS, ROW_WORDS)],
                                         sem.at[dst_slot])
        _issue_row_copies(GATHER_ROWS, make_copy)

    @pl.when(jnp.logical_and(i == 0, n_used > 0))
    def _():
        issue_from(tok_ref, 0)

    @pl.when(i + 1 < n_used)
    def _():
        issue_from(tokn_ref, 1 - slot)

    @pl.when(i < n_used)
    def _():
        pltpu.make_async_copy(hp_ref.at[pl.ds(0, GATHER_ROWS * ROW_WORDS)], buf_ref.at[slot],
                              sem.at[slot]).wait()
        xs_ref[...] = _unpack_rows(buf_ref.at[slot], GATHER_ROWS)

    @pl.when(i >= n_used)
    def _():
        xs_ref[...] = jnp.zeros_like(xs_ref)


def _gather_rows(n_used, slot_tok, hp, d):
    n_slots = slot_tok.shape[0]
    n_steps = n_slots // GATHER_ROWS
    tok3 = slot_tok.reshape(n_steps, 1, GATHER_ROWS)
    return pl.pallas_call(
        _gather_kernel,
        out_shape=jax.ShapeDtypeStruct((n_slots, d), BF16),
        grid_spec=pltpu.PrefetchScalarGridSpec(
            num_scalar_prefetch=1,
            grid=(n_steps,),
            in_specs=[
                pl.BlockSpec((1, 1, GATHER_ROWS), lambda i, nu: (i, 0, 0), memory_space=pltpu.SMEM),
                pl.BlockSpec((1, 1, GATHER_ROWS),
                             lambda i, nu: (jnp.minimum(i + 1, n_steps - 1), 0, 0),
                             memory_space=pltpu.SMEM),
                pl.BlockSpec(memory_space=pl.ANY),
            ],
            out_specs=pl.BlockSpec((GATHER_ROWS, d), lambda i, nu: (i, 0)),
            scratch_shapes=[pltpu.VMEM((2, GATHER_ROWS * ROW_WORDS, LANES), jnp.uint32),
                            pltpu.SemaphoreType.DMA((2,))],
        ),
        compiler_params=_cparams("arbitrary"),
        name="gather_rows",
    )(n_used, tok3, tok3, hp)


def _block_state(be_ref):
    bi = pl.program_id(1)
    used = bi < be_ref[pl.num_programs(1)]
    prev = be_ref[jnp.maximum(bi - 1, 0)]
    return used, jnp.logical_or(bi == 0, be_ref[bi] != prev)


def _gmm_gu_kernel(be_ref, x_ref, wg_ref, wl_ref, bg_ref, bl_ref, act_ref, wg_s, wl_s):
    used, changed = _block_state(be_ref)

    @pl.when(jnp.logical_and(used, changed))
    def _():
        wg_s[...] = wg_ref[...].astype(BF16)
        wl_s[...] = wl_ref[...].astype(BF16)

    @pl.when(used)
    def _():
        xb = x_ref[...]
        glu = jnp.dot(xb, wg_s[...], preferred_element_type=F32) + bg_ref[...]
        lin = jnp.dot(xb, wl_s[...], preferred_element_type=F32) + bl_ref[...]
        glu = jnp.minimum(glu, SWIGLU_LIMIT)
        lin = jnp.clip(lin, -SWIGLU_LIMIT, SWIGLU_LIMIT)
        sig = 1.0 / (1.0 + jnp.exp(-SWIGLU_ALPHA * glu))
        act_ref[...] = (glu * sig * (lin + 1.0)).astype(BF16)

    @pl.when(jnp.logical_not(used))
    def _():
        act_ref[...] = jnp.zeros_like(act_ref)


def _gmm_gu(block_e, xs, w_gu, b_gu4, layer):
    n_slots, d = xs.shape
    f = w_gu.shape[3] // 2
    tm, tn = EXPERT_TM, 1024
    nn = f // tn
    return pl.pallas_call(
        _gmm_gu_kernel,
        out_shape=jax.ShapeDtypeStruct((n_slots, f), BF16),
        grid_spec=pltpu.PrefetchScalarGridSpec(
            num_scalar_prefetch=1,
            grid=(nn, n_slots // tm),
            in_specs=[
                pl.BlockSpec((tm, d), lambda j, i, be: (i, 0)),
                pl.BlockSpec((None, None, d, tn), lambda j, i, be: (layer, be[i], 0, j)),
                pl.BlockSpec((None, None, d, tn), lambda j, i, be: (layer, be[i], 0, nn + j)),
                pl.BlockSpec((None, None, 1, tn), lambda j, i, be: (layer, be[i], 0, j)),
                pl.BlockSpec((None, None, 1, tn), lambda j, i, be: (layer, be[i], 0, nn + j)),
            ],
            out_specs=pl.BlockSpec((tm, tn), lambda j, i, be: (i, j)),
            scratch_shapes=[pltpu.VMEM((d, tn), BF16), pltpu.VMEM((d, tn), BF16)],
        ),
        compiler_params=_cparams("arbitrary", "arbitrary"),
        name="gmm_gate_up",
    )(block_e, xs, w_gu, w_gu, b_gu4, b_gu4)


DOWN_TN = SUBLANES * LANES


def _gmm_down_kernel(be_ref, a_ref, w_ref, b_ref, y_ref, w_s):
    used, changed = _block_state(be_ref)

    @pl.when(jnp.logical_and(used, changed))
    def _():
        w_s[...] = w_ref[...].astype(BF16)

    @pl.when(used)
    def _():
        y = jnp.dot(a_ref[...], w_s[...], preferred_element_type=F32) + b_ref[...]
        tm = y.shape[0]
        y2_ref = y_ref.reshape(tm * SUBLANES, LANES)
        for cc in range(SUBLANES):
            y2_ref[pl.ds(cc, tm, stride=SUBLANES), :] = y[:, cc * LANES:(cc + 1) * LANES]

    @pl.when(jnp.logical_not(used))
    def _():
        y_ref[...] = jnp.zeros_like(y_ref)


def _gmm_down(block_e, act, w_down, b_down4, layer):
    n_slots, f = act.shape
    d = w_down.shape[3]
    tm, tn = EXPERT_TM, DOWN_TN
    return pl.pallas_call(
        _gmm_down_kernel,
        out_shape=jax.ShapeDtypeStruct((n_slots, d // LANES, LANES), F32),
        grid_spec=pltpu.PrefetchScalarGridSpec(
            num_scalar_prefetch=1,
            grid=(d // tn, n_slots // tm),
            in_specs=[
                pl.BlockSpec((tm, f), lambda j, i, be: (i, 0)),
                pl.BlockSpec((None, None, f, tn), lambda j, i, be: (layer, be[i], 0, j)),
                pl.BlockSpec((None, None, 1, tn), lambda j, i, be: (layer, be[i], 0, j)),
            ],
            out_specs=pl.BlockSpec((tm, SUBLANES, LANES), lambda j, i, be: (i, j, 0)),
            scratch_shapes=[pltpu.VMEM((f, tn), BF16)],
        ),
        compiler_params=_cparams("arbitrary", "arbitrary"),
        name="gmm_down",
    )(block_e, act, w_down, b_down4)


COMBINE_ROWS = COMBINE_TOK * TOP_K
COMBINE_SUB = 32


def _combine_kernel(pos_ref, posn_ref, ys_ref, gate_ref, x_ref, g2_ref, gf_ref, out_ref,
                    buf_ref, sem, *, final_norm):
    i = pl.program_id(0)
    n = pl.num_programs(0)
    slot = lax.rem(i, 2)
    n_chunks = x_ref.shape[1] // LANES

    def issue_from(p_ref, dst_slot):
        def make_copy(r):
            src = pl.multiple_of(p_ref[0, 0, r] * n_chunks, n_chunks)
            return pltpu.make_async_copy(ys_ref.at[pl.ds(src, n_chunks)],
                                         buf_ref.at[dst_slot, pl.ds(r * n_chunks, n_chunks)],
                                         sem.at[dst_slot])
        _issue_row_copies(COMBINE_ROWS, make_copy)

    @pl.when(i == 0)
    def _():
        issue_from(pos_ref, 0)

    @pl.when(i + 1 < n)
    def _():
        issue_from(posn_ref, 1 - slot)

    pltpu.make_async_copy(ys_ref.at[pl.ds(0, COMBINE_ROWS * n_chunks)], buf_ref.at[slot],
                          sem.at[slot]).wait()

    rows_ref = buf_ref.at[slot]
    d = n_chunks * LANES
    for t0 in range(0, COMBINE_TOK, COMBINE_SUB):
        tok = slice(t0, t0 + COMBINE_SUB)
        gates = [jnp.broadcast_to(gate_ref[tok, k:k + 1], (COMBINE_SUB, LANES)) for k in range(TOP_K)]
        ssq = jnp.zeros((COMBINE_SUB, LANES), F32)
        for c in range(n_chunks):
            cols = slice(c * LANES, (c + 1) * LANES)
            y = None
            for k in range(TOP_K):
                first = (k * COMBINE_TOK + t0) * n_chunks + c
                rows = rows_ref[pl.ds(first, COMBINE_SUB, stride=n_chunks), :]
                y = gates[k] * rows if y is None else y + gates[k] * rows
            o = x_ref[tok, cols] + g2_ref[0, :, cols] * y
            out_ref[tok, cols] = o
            ssq = ssq + o * o
        if final_norm:
            inv = lax.rsqrt(jnp.sum(ssq, axis=-1, keepdims=True) / d + RMS_EPS)
            out_ref[tok, :] = out_ref[tok, :] * inv * gf_ref[...]


def _combine(pos, ys2, gate, x2, mod3, gate_chunk, tokens_per_batch, final_g, final_norm):
    t, d = x2.shape
    n_chunks = d // LANES
    n_steps = t // COMBINE_TOK
    steps_per_batch = tokens_per_batch // COMBINE_TOK
    pos3 = pos.reshape(n_steps, COMBINE_TOK, TOP_K).transpose(0, 2, 1).reshape(n_steps, 1, COMBINE_ROWS)
    return pl.pallas_call(
        functools.partial(_combine_kernel, final_norm=final_norm),
        out_shape=jax.ShapeDtypeStruct((t, d), F32),
        grid=(n_steps,),
        in_specs=[
            pl.BlockSpec((1, 1, COMBINE_ROWS), lambda i: (i, 0, 0), memory_space=pltpu.SMEM),
            pl.BlockSpec((1, 1, COMBINE_ROWS), lambda i: (jnp.minimum(i + 1, n_steps - 1), 0, 0),
                         memory_space=pltpu.SMEM),
            pl.BlockSpec(memory_space=pl.ANY),
            pl.BlockSpec((COMBINE_TOK, TOP_K), lambda i: (i, 0)),
            pl.BlockSpec((COMBINE_TOK, d), lambda i: (i, 0)),
            pl.BlockSpec((1, 1, d), lambda i: (i // steps_per_batch, 0, gate_chunk)),
            pl.BlockSpec((1, d), lambda i: (0, 0)),
        ],
        out_specs=pl.BlockSpec((COMBINE_TOK, d), lambda i: (i, 0)),
        scratch_shapes=[pltpu.VMEM((2, COMBINE_ROWS * n_chunks, LANES), F32),
                        pltpu.SemaphoreType.DMA((2,))],
        compiler_params=_cparams("arbitrary"),
        name="combine",
    )(pos3, pos3, ys2, gate, x2, mod3, final_g)


SCATTER_CHUNK = 8192
TOP_K_SHIFT = 2
assert 1 << TOP_K_SHIFT == TOP_K and DMA_UNROLL % TOP_K == 0


def _slot_tok_kernel(pos_ref, zeros_ref, out_ref):
    i = pl.program_id(0)

    @pl.when(i == 0)
    def _():
        pltpu.sync_copy(zeros_ref, out_ref)

    base = i * SCATTER_CHUNK

    def scatter(it, carry):
        a0 = it * DMA_UNROLL
        tok0 = lax.shift_right_logical(base + a0, TOP_K_SHIFT)
        for u in range(DMA_UNROLL):
            out_ref[pos_ref[0, 0, a0 + u]] = tok0 + (u >> TOP_K_SHIFT)
        return carry

    lax.fori_loop(0, SCATTER_CHUNK // DMA_UNROLL, scatter, 0)


def _slot_tokens(pos, n_slots):
    n_steps = pos.shape[0] // SCATTER_CHUNK
    return pl.pallas_call(
        _slot_tok_kernel,
        out_shape=jax.ShapeDtypeStruct((n_slots,), jnp.int32),
        grid=(n_steps,),
        in_specs=[pl.BlockSpec((1, 1, SCATTER_CHUNK), lambda i: (i, 0, 0),
                               memory_space=pltpu.SMEM),
                  pl.BlockSpec(memory_space=pl.ANY)],
        out_specs=pl.BlockSpec(memory_space=pltpu.SMEM),
        compiler_params=_cparams("arbitrary"),
        name="slot_tokens",
    )(pos.reshape(n_steps, 1, SCATTER_CHUNK), jnp.zeros((n_slots,), jnp.int32))


def _slot_plan(top_idx, rank, expert_counts):
    t = top_idx.shape[0]
    n_assign = t * TOP_K
    flat_e = top_idx.reshape(-1)
    counts = expert_counts[0, :N_EXPERTS].astype(jnp.int32)
    rank = rank.reshape(-1)
    padded = (counts + EXPERT_TM - 1) // EXPERT_TM * EXPERT_TM
    padded_end = jnp.cumsum(padded)
    padded_start = padded_end - padded
    pos = (padded_start[flat_e] + rank).astype(jnp.int32)
    n_blocks = -(-(n_assign + N_EXPERTS * (EXPERT_TM - 1)) // EXPERT_TM)
    n_blocks = -(-n_blocks * EXPERT_TM // GATHER_ROWS) * GATHER_ROWS // EXPERT_TM
    n_slots = n_blocks * EXPERT_TM
    slot_tok = _slot_tokens(pos, n_slots)
    block_start = jnp.arange(n_blocks, dtype=jnp.int32) * EXPERT_TM
    block_e = jnp.minimum(jnp.sum(block_start[:, None] >= padded_end[None, :], axis=1),
                          N_EXPERTS - 1).astype(jnp.int32)
    n_used = (padded_end[-1:] // EXPERT_TM).astype(jnp.int32)
    return pos, slot_tok, jnp.concatenate([block_e, n_used]), n_used


def _expert_ffn_residual(x, routed, mod3, layer, w_gu, b_gu4, w_down, b_down4, final_g,
                         final_norm):
    b, l, d = x.shape
    assert d == 2 * ROW_WORDS * LANES
    hp, top_idx, gate, rank, expert_counts = routed
    pos, slot_tok, block_e, n_used = _slot_plan(top_idx, rank, expert_counts)
    xs = _gather_rows(n_used, slot_tok, hp, d)
    act = _gmm_gu(block_e, xs, w_gu, b_gu4, layer)
    ys3 = _gmm_down(block_e, act, w_down, b_down4, layer)
    ys2 = ys3.reshape(ys3.shape[0] * ys3.shape[1], LANES)
    out = _combine(pos, ys2, gate, x.reshape(b * l, d), mod3, 5, l, final_g, final_norm)
    return out.reshape(b, l, d)


def _rope_tables(length):
    rows = length // GRID_W
    row = jnp.repeat(jnp.arange(rows), GRID_W).astype(F32)
    col = jnp.tile(jnp.arange(GRID_W), rows).astype(F32)
    inv = 1.0 / (ROPE_THETA ** (jnp.arange(0, ROPE_AXIS, 2, dtype=F32) / ROPE_AXIS))
    ang_r = row[:, None] * inv[None, :]
    ang_c = col[:, None] * inv[None, :]
    cos = jnp.concatenate([jnp.cos(ang_r)] * 2 + [jnp.cos(ang_c)] * 2, axis=-1)
    sin = jnp.concatenate([jnp.sin(ang_r)] * 2 + [jnp.sin(ang_c)] * 2, axis=-1)
    return cos, sin


def _rotate_half_cols(w):
    a, b_, c_, d_ = jnp.split(w, 4, axis=-1)
    return jnp.concatenate([-b_, a, -d_, c_], axis=-1)


def kernel(x, c, ctx, c_ctx, ada_w, ada_b, norm_mix_g, norm_ffn_g, mla_w_in, mla_q_norm_g,
           mla_kv_norm_g, mla_w_q_up, mla_w_kv_up, mla_w_out, conv_w_in, conv_w, conv_w_out,
           router_w, router_b, expert_w_gu, expert_b_gu, expert_w_down, expert_b_down,
           final_norm_g):
    b, l, d = x.shape
    depth = ada_w.shape[0]
    lc = ctx.shape[1]
    assert depth == 2 and b + 1 <= ADA_ROWS

    cv = jnp.concatenate([c, c_ctx[None, :], jnp.zeros((ADA_ROWS - b - 1, d), F32)], axis=0)
    ctx_row = b
    ada_b3 = ada_b.reshape(depth, 1, N_ADA * d)
    cos, sin = _rope_tables(l)
    b_gu4 = expert_b_gu.reshape(depth, N_EXPERTS, 1, -1)
    b_down4 = expert_b_down.reshape(depth, N_EXPERTS, 1, d)
    w_r = jnp.pad(router_w, ((0, 0), (0, 0), (0, LANES - N_EXPERTS)))
    b_r = jnp.pad(router_b, ((0, 0), (0, LANES - N_EXPERTS)))[:, None, :]
    final_g = final_norm_g[None, :]

    mod3 = _ada(cv, ada_w, ada_b3, 0).reshape(ADA_ROWS, 1, N_ADA * d)
    w_in = mla_w_in[0]
    w_kr = w_in[:, Q_LORA + KV_LORA:]
    w_all = jnp.concatenate([w_in, _rotate_half_cols(w_kr)], axis=1).astype(BF16)
    w_ctx = w_in[:, Q_LORA:].astype(BF16)
    gq = mla_q_norm_g[0][None, :]
    gkv = mla_kv_norm_g[0][None, :]
    g_mix = norm_mix_g[0][None, :]
    wq = mla_w_q_up[0].reshape(Q_LORA, N_HEADS, QK_DIM)
    wq_rope = wq[..., QK_NOPE:]
    wq_all = jnp.concatenate([wq, _rotate_half_cols(wq_rope)], axis=-1)
    wq_all = wq_all.reshape(Q_LORA, N_HEADS * Q_HEAD_COLS).astype(BF16)
    wkv = mla_w_kv_up[0].astype(BF16)
    w_o = mla_w_out[0].astype(BF16)

    cq, ckv, kr = _mla_in(x, g_mix, mod3, 0, w_all, gq, gkv, cos, sin)
    ckv_c, kr_c = _mla_in_ctx(ctx, g_mix, mod3, ctx_row, w_ctx, gkv)
    ckv_all = jnp.concatenate([ckv_c, ckv], axis=1)
    kr_all = jnp.concatenate([kr_c, kr], axis=1)
    q = _q_up(cq, wq_all, cos, sin)
    k, v = _kv_up(ckv_all, kr_all, wkv, 768)
    o = _attention(q, k, v, 1024, 768)
    x, routed = _proj_res(o, w_o, x, mod3, 2, norm_ffn_g[0][None, :], w_r[0], b_r[0])
    x = _expert_ffn_residual(x, routed, mod3, 0, expert_w_gu, b_gu4, expert_w_down, b_down4,
                             final_g, False)

    mod3 = _ada(cv, ada_w, ada_b3, 1).reshape(ADA_ROWS, 1, N_ADA * d)
    gb, z = _conv_in(x, norm_mix_g[1][None, :], mod3, 0, conv_w_in[0].astype(BF16))
    x, routed = _conv_out(z, gb, conv_w[0], conv_w_out[0].astype(BF16), x, mod3, 2,
                          norm_ffn_g[1][None, :], w_r[1], b_r[1])
    x = _expert_ffn_residual(x, routed, mod3, 1, expert_w_gu, b_gu4, expert_w_down, b_down4,
                             final_g, True)
    return x
```

```python
import functools

import jax
import jax.numpy as jnp
import numpy as np
from jax import lax
from jax.experimental import pallas as pl
from jax.experimental.pallas import tpu as pltpu

GRID_W = 64
RMS_EPS = 1e-6
N_ADA = 6
N_HEADS = 16
Q_LORA = 512
KV_LORA = 512
QK_NOPE = 128
QK_ROPE = 64
V_DIM = 128
ROPE_AXIS = QK_ROPE // 2
ROPE_THETA = 10000.0
QK_DIM = QK_NOPE + QK_ROPE
ATTN_SCALE = QK_DIM ** -0.5
LOG2_E = 1.4426950408889634
Q_SCALE = ATTN_SCALE * LOG2_E
N_EXPERTS = 32
TOP_K = 4
SWIGLU_LIMIT = 7.0
SWIGLU_ALPHA = 1.702

LANES = 128
SUBLANES = 8
VMEM_LIMIT = 56 * 1024 * 1024

ADA_ROWS = 8
EXPERT_TM = 512
GATHER_ROWS = EXPERT_TM
COMBINE_TOK = 128
ROW_WORDS = SUBLANES
DMA_UNROLL = 8

BF16 = jnp.bfloat16
F32 = jnp.float32


def _cparams(*sem):
    return pltpu.CompilerParams(dimension_semantics=sem, vmem_limit_bytes=VMEM_LIMIT)


def _rms(x):
    return x * lax.rsqrt(jnp.mean(x * x, axis=-1, keepdims=True) + RMS_EPS)


def _modulate(x, g, shift, scale):
    return _rms(x) * g * (1.0 + scale) + shift


def _ada_kernel(cv_ref, w_ref, b_ref, o_ref):
    cv = cv_ref[...]
    s = cv * (1.0 / (1.0 + jnp.exp(-cv)))
    o_ref[...] = jnp.dot(s, w_ref[...], preferred_element_type=F32,
                         precision=lax.Precision.HIGHEST) + b_ref[...]


def _ada(cv, ada_w, ada_b3, layer):
    d = cv.shape[1]
    n = ada_w.shape[2]
    tn = 1024
    return pl.pallas_call(
        _ada_kernel,
        out_shape=jax.ShapeDtypeStruct((ADA_ROWS, n), F32),
        grid=(n // tn,),
        in_specs=[
            pl.BlockSpec((ADA_ROWS, d), lambda j: (0, 0)),
            pl.BlockSpec((None, d, tn), lambda j: (layer, 0, j)),
            pl.BlockSpec((None, 1, tn), lambda j: (layer, 0, j)),
        ],
        out_specs=pl.BlockSpec((ADA_ROWS, tn), lambda j: (0, j)),
        compiler_params=_cparams("arbitrary"),
        name="ada",
    )(cv, ada_w, ada_b3)


def _mla_in_kernel(x_ref, g_ref, sh_ref, sc_ref, w_ref, gq_ref, gkv_ref, cos_ref, sin_ref,
                   cq_ref, ckv_ref, kr_ref):
    h = _modulate(x_ref[0], g_ref[...], sh_ref[0], sc_ref[0]).astype(BF16)
    p = jnp.dot(h, w_ref[...], preferred_element_type=F32)
    cq_ref[0] = (_rms(p[:, :Q_LORA]) * gq_ref[...]).astype(BF16)
    ckv_ref[0] = (_rms(p[:, Q_LORA:Q_LORA + KV_LORA]) * gkv_ref[...]).astype(BF16)
    o = Q_LORA + KV_LORA
    kr = p[:, o:o + QK_ROPE]
    kr_rot = p[:, o + QK_ROPE:o + 2 * QK_ROPE]
    kr_ref[0] = (kr * cos_ref[...] + kr_rot * sin_ref[...]).astype(BF16)


def _mla_in_ctx_kernel(x_ref, g_ref, sh_ref, sc_ref, w_ref, gkv_ref, ckv_ref, kr_ref):
    h = _modulate(x_ref[0], g_ref[...], sh_ref[0], sc_ref[0]).astype(BF16)
    p = jnp.dot(h, w_ref[...], preferred_element_type=F32)
    ckv_ref[0] = (_rms(p[:, :KV_LORA]) * gkv_ref[...]).astype(BF16)
    kr_ref[0] = p[:, KV_LORA:KV_LORA + QK_ROPE].astype(BF16)


def _mla_in(x, g, mod3, layer_chunk0, w_all, gq, gkv, cos, sin):
    b, l, d = x.shape
    tm = 512
    nw = w_all.shape[1]
    row = lambda bi, i: (bi, i, 0)
    return pl.pallas_call(
        _mla_in_kernel,
        out_shape=(jax.ShapeDtypeStruct((b, l, Q_LORA), BF16),
                   jax.ShapeDtypeStruct((b, l, KV_LORA), BF16),
                   jax.ShapeDtypeStruct((b, l, QK_ROPE), BF16)),
        grid=(b, l // tm),
        in_specs=[
            pl.BlockSpec((1, tm, d), row),
            pl.BlockSpec((1, d), lambda bi, i: (0, 0)),
            pl.BlockSpec((1, 1, d), lambda bi, i: (bi, 0, layer_chunk0)),
            pl.BlockSpec((1, 1, d), lambda bi, i: (bi, 0, layer_chunk0 + 1)),
            pl.BlockSpec((d, nw), lambda bi, i: (0, 0)),
            pl.BlockSpec((1, Q_LORA), lambda bi, i: (0, 0)),
            pl.BlockSpec((1, KV_LORA), lambda bi, i: (0, 0)),
            pl.BlockSpec((tm, QK_ROPE), lambda bi, i: (i, 0)),
            pl.BlockSpec((tm, QK_ROPE), lambda bi, i: (i, 0)),
        ],
        out_specs=(pl.BlockSpec((1, tm, Q_LORA), row),
                   pl.BlockSpec((1, tm, KV_LORA), row),
                   pl.BlockSpec((1, tm, QK_ROPE), row)),
        compiler_params=_cparams("arbitrary", "arbitrary"),
        name="mla_in",
    )(x, g, mod3, mod3, w_all, gq, gkv, cos, sin)


def _mla_in_ctx(ctx, g, mod3, ctx_row, w_kv, gkv):
    b, l, d = ctx.shape
    tm = l
    nw = w_kv.shape[1]
    row = lambda bi, i: (bi, i, 0)
    return pl.pallas_call(
        _mla_in_ctx_kernel,
        out_shape=(jax.ShapeDtypeStruct((b, l, KV_LORA), BF16),
                   jax.ShapeDtypeStruct((b, l, QK_ROPE), BF16)),
        grid=(b, l // tm),
        in_specs=[
            pl.BlockSpec((1, tm, d), row),
            pl.BlockSpec((1, d), lambda bi, i: (0, 0)),
            pl.BlockSpec((1, 1, d), lambda bi, i: (ctx_row, 0, 0)),
            pl.BlockSpec((1, 1, d), lambda bi, i: (ctx_row, 0, 1)),
            pl.BlockSpec((d, nw), lambda bi, i: (0, 0)),
            pl.BlockSpec((1, KV_LORA), lambda bi, i: (0, 0)),
        ],
        out_specs=(pl.BlockSpec((1, tm, KV_LORA), row),
                   pl.BlockSpec((1, tm, QK_ROPE), row)),
        compiler_params=_cparams("arbitrary", "arbitrary"),
        name="mla_in_ctx",
    )(ctx, g, mod3, mod3, w_kv, gkv)


Q_HEAD_COLS = QK_NOPE + 2 * QK_ROPE
KV_HEAD_COLS = QK_NOPE + V_DIM


def _q_up_kernel(cq_ref, w_ref, cos_ref, sin_ref, q_ref):
    cq = cq_ref[0]
    cos = cos_ref[...]
    sin = sin_ref[...]
    for h in range(N_HEADS):
        p = jnp.dot(cq, w_ref[:, h * Q_HEAD_COLS:(h + 1) * Q_HEAD_COLS],
                    preferred_element_type=F32)
        rope = p[:, QK_NOPE:QK_NOPE + QK_ROPE] * cos + p[:, QK_NOPE + QK_ROPE:] * sin
        q_ref[0, h, :, :QK_NOPE] = (p[:, :QK_NOPE] * Q_SCALE).astype(BF16)
        q_ref[0, h, :, QK_NOPE:] = (rope * Q_SCALE).astype(BF16)


def _q_up(cq, w_q, cos, sin):
    b, l, _ = cq.shape
    tm = 512
    return pl.pallas_call(
        _q_up_kernel,
        out_shape=jax.ShapeDtypeStruct((b, N_HEADS, l, QK_DIM), BF16),
        grid=(b, l // tm),
        in_specs=[
            pl.BlockSpec((1, tm, Q_LORA), lambda bi, i: (bi, i, 0)),
            pl.BlockSpec(w_q.shape, lambda bi, i: (0, 0)),
            pl.BlockSpec((tm, QK_ROPE), lambda bi, i: (i, 0)),
            pl.BlockSpec((tm, QK_ROPE), lambda bi, i: (i, 0)),
        ],
        out_specs=pl.BlockSpec((1, N_HEADS, tm, QK_DIM), lambda bi, i: (bi, 0, i, 0)),
        compiler_params=_cparams("arbitrary", "arbitrary"),
        name="q_up",
    )(cq, w_q, cos, sin)


def _kv_up_kernel(ckv_ref, kr_ref, w_ref, k_ref, v_ref):
    ckv = ckv_ref[0]
    kr = kr_ref[0]
    for h in range(N_HEADS):
        p = jnp.dot(ckv, w_ref[:, h * KV_HEAD_COLS:(h + 1) * KV_HEAD_COLS],
                    preferred_element_type=F32)
        k_ref[0, h, :, :QK_NOPE] = p[:, :QK_NOPE].astype(BF16)
        k_ref[0, h, :, QK_NOPE:] = kr
        v_ref[0, h] = p[:, QK_NOPE:].astype(BF16)


def _kv_up(ckv, kr, w_kv, tm):
    b, l, _ = ckv.shape
    return pl.pallas_call(
        _kv_up_kernel,
        out_shape=(jax.ShapeDtypeStruct((b, N_HEADS, l, QK_DIM), BF16),
                   jax.ShapeDtypeStruct((b, N_HEADS, l, V_DIM), BF16)),
        grid=(b, l // tm),
        in_specs=[
            pl.BlockSpec((1, tm, KV_LORA), lambda bi, i: (bi, i, 0)),
            pl.BlockSpec((1, tm, QK_ROPE), lambda bi, i: (bi, i, 0)),
            pl.BlockSpec(w_kv.shape, lambda bi, i: (0, 0)),
        ],
        out_specs=(pl.BlockSpec((1, N_HEADS, tm, QK_DIM), lambda bi, i: (bi, 0, i, 0)),
                   pl.BlockSpec((1, N_HEADS, tm, V_DIM), lambda bi, i: (bi, 0, i, 0))),
        compiler_params=_cparams("arbitrary", "arbitrary"),
        name="kv_up",
    )(ckv, kr, w_kv)


def _attn_kernel(q_ref, k_ref, v_ref, o_ref, *, tk, n_chunks):
    q = q_ref[0, 0]
    tq = q.shape[0]
    n_slabs = tk // LANES
    m = jnp.full((tq, LANES), -jnp.inf, F32)
    l_loc = jnp.zeros((tq, LANES), F32)
    acc = jnp.zeros((tq, V_DIM), F32)
    for c in range(n_chunks):
        k = k_ref[0, 0, c * tk:(c + 1) * tk, :]
        s = lax.dot_general(q, k, (((1,), (1,)), ((), ())), preferred_element_type=F32)
        slabs = [s[:, j * LANES:(j + 1) * LANES] for j in range(n_slabs)]
        m_loc = slabs[0]
        for slab in slabs[1:]:
            m_loc = jnp.maximum(m_loc, slab)
        m_new = jnp.maximum(m, jnp.max(m_loc, axis=-1, keepdims=True))
        alpha = jnp.exp2(m - m_new)
        ps = [jnp.exp2(slab - m_new) for slab in slabs]
        p_sum = ps[0]
        for p in ps[1:]:
            p_sum = p_sum + p
        l_loc = alpha * l_loc + p_sum
        p_bf = jnp.concatenate([p.astype(BF16) for p in ps], axis=-1)
        acc = alpha * acc + jnp.dot(p_bf, v_ref[0, 0, c * tk:(c + 1) * tk, :],
                                    preferred_element_type=F32)
        m = m_new
    o_ref[0] = (acc / jnp.sum(l_loc, axis=-1, keepdims=True)).astype(BF16)


def _attention(q, k, v, tq, tk):
    b, h, l, _ = q.shape
    lk = k.shape[2]
    return pl.pallas_call(
        functools.partial(_attn_kernel, tk=tk, n_chunks=lk // tk),
        out_shape=jax.ShapeDtypeStruct((b, l, h * V_DIM), BF16),
        grid=(b, h, l // tq),
        in_specs=[
            pl.BlockSpec((1, 1, tq, QK_DIM), lambda bi, hi, i: (bi, hi, i, 0)),
            pl.BlockSpec((1, 1, lk, QK_DIM), lambda bi, hi, i: (bi, hi, 0, 0)),
            pl.BlockSpec((1, 1, lk, V_DIM), lambda bi, hi, i: (bi, hi, 0, 0)),
        ],
        out_specs=pl.BlockSpec((1, tq, V_DIM), lambda bi, hi, i: (bi, i, hi)),
        compiler_params=_cparams("arbitrary", "arbitrary", "arbitrary"),
        name="attention",
    )(q, k, v)


def _proj_res_kernel(o_ref, w_ref, x_ref, g1_ref, *refs):
    route_in, (out_ref, *route_out) = refs[:N_ROUTE_IN], refs[N_ROUTE_IN:]
    mix = jnp.dot(o_ref[0], w_ref[...], preferred_element_type=F32)
    x1 = x_ref[0] + g1_ref[0] * mix
    out_ref[0] = x1
    _route_tile(x1, *route_in, *route_out)


def _proj_res(o, w_out, x, mod3, gate_chunk, norm_ffn_g, w_r, b_r):
    b, l, d = x.shape
    tm = 512
    row = lambda bi, i: (bi, i, 0)
    r_in, r_specs, r_shapes, r_out_specs = _route_specs(b, l, d, tm, norm_ffn_g, mod3, 3, w_r, b_r)
    x1, *routed = pl.pallas_call(
        _proj_res_kernel,
        out_shape=(jax.ShapeDtypeStruct((b, l, d), F32), *r_shapes),
        grid=(b, l // tm),
        in_specs=[
            pl.BlockSpec((1, tm, o.shape[2]), row),
            pl.BlockSpec(w_out.shape, lambda bi, i: (0, 0)),
            pl.BlockSpec((1, tm, d), row),
            pl.BlockSpec((1, 1, d), lambda bi, i: (bi, 0, gate_chunk)),
            *r_specs,
        ],
        out_specs=(pl.BlockSpec((1, tm, d), row), *r_out_specs),
        compiler_params=_cparams("arbitrary", "arbitrary"),
        name="proj_res",
    )(o, w_out, x, mod3, *r_in)
    return x1, routed


def _conv_in_kernel(x_ref, g_ref, sh_ref, sc_ref, wb_ref, wc_ref, wu_ref, gb_ref, z_ref, h_ref):
    @pl.when(pl.program_id(2) == 0)
    def _():
        h_ref[...] = _modulate(x_ref[0], g_ref[...], sh_ref[0], sc_ref[0]).astype(BF16)

    h = h_ref[...]
    gb_ref[0] = jnp.dot(h, wb_ref[...], preferred_element_type=F32).astype(BF16)
    gc = jnp.dot(h, wc_ref[...], preferred_element_type=F32)
    u = jnp.dot(h, wu_ref[...], preferred_element_type=F32)
    z_ref[0] = (gc * u).astype(BF16)


def _conv_in(x, g, mod3, chunk0, w_in):
    b, l, d = x.shape
    tm, tn = 512, 1024
    nn = d // tn
    row = lambda bi, i, j: (bi, i, 0)
    col = lambda bi, i, j: (bi, i, j)
    return pl.pallas_call(
        _conv_in_kernel,
        out_shape=(jax.ShapeDtypeStruct((b, l, d), BF16), jax.ShapeDtypeStruct((b, l, d), BF16)),
        grid=(b, l // tm, nn),
        in_specs=[
            pl.BlockSpec((1, tm, d), row),
            pl.BlockSpec((1, d), lambda bi, i, j: (0, 0)),
            pl.BlockSpec((1, 1, d), lambda bi, i, j: (bi, 0, chunk0)),
            pl.BlockSpec((1, 1, d), lambda bi, i, j: (bi, 0, chunk0 + 1)),
            pl.BlockSpec((d, tn), lambda bi, i, j: (0, j)),
            pl.BlockSpec((d, tn), lambda bi, i, j: (0, nn + j)),
            pl.BlockSpec((d, tn), lambda bi, i, j: (0, 2 * nn + j)),
        ],
        out_specs=(pl.BlockSpec((1, tm, tn), col), pl.BlockSpec((1, tm, tn), col)),
        scratch_shapes=[pltpu.VMEM((tm, d), BF16)],
        compiler_params=_cparams("arbitrary", "arbitrary", "arbitrary"),
        name="conv_in",
    )(x, g, mod3, mod3, w_in, w_in, w_in)


def _conv_out_kernel(z_ref, zp_ref, zn_ref, gb_ref, cw_ref, w_ref, x_ref, g1_ref, *refs):
    route_in, (out_ref, *route_out) = refs[:N_ROUTE_IN], refs[N_ROUTE_IN:]
    i = pl.program_id(1)
    last = pl.num_programs(1) - 1
    z = z_ref[0].astype(F32)
    tm = z.shape[0]
    prev_row = jnp.where(i > 0, zp_ref[0, SUBLANES - 1:SUBLANES, :].astype(F32), 0.0)
    next_row = jnp.where(i < last, zn_ref[0, 0:1, :].astype(F32), 0.0)
    ridx = lax.broadcasted_iota(jnp.int32, z.shape, 0)
    z_prev = jnp.where(ridx == 0, prev_row, pltpu.roll(z, 1, 0))
    z_next = jnp.where(ridx == tm - 1, next_row, pltpu.roll(z, tm - 1, 0))
    conv = cw_ref[0:1, :] * z_prev + cw_ref[1:2, :] * z + cw_ref[2:3, :] * z_next
    y = (gb_ref[0].astype(F32) * conv).astype(BF16)
    mix = jnp.dot(y, w_ref[...], preferred_element_type=F32)
    x1 = x_ref[0] + g1_ref[0] * mix
    out_ref[0] = x1
    _route_tile(x1, *route_in, *route_out)


def _conv_out(z, gb, conv_w, w_out, x, mod3, gate_chunk, norm_ffn_g, w_r, b_r):
    b, l, d = x.shape
    tm = 512
    hb = tm // SUBLANES
    n_halo = l // SUBLANES
    row = lambda bi, i: (bi, i, 0)
    r_in, r_specs, r_shapes, r_out_specs = _route_specs(b, l, d, tm, norm_ffn_g, mod3, 3, w_r, b_r)
    x1, *routed = pl.pallas_call(
        _conv_out_kernel,
        out_shape=(jax.ShapeDtypeStruct((b, l, d), F32), *r_shapes),
        grid=(b, l // tm),
        in_specs=[
            pl.BlockSpec((1, tm, d), row),
            pl.BlockSpec((1, SUBLANES, d), lambda bi, i: (bi, jnp.maximum(i * hb - 1, 0), 0)),
            pl.BlockSpec((1, SUBLANES, d),
                         lambda bi, i: (bi, jnp.minimum((i + 1) * hb, n_halo - 1), 0)),
            pl.BlockSpec((1, tm, d), row),
            pl.BlockSpec(conv_w.shape, lambda bi, i: (0, 0)),
            pl.BlockSpec(w_out.shape, lambda bi, i: (0, 0)),
            pl.BlockSpec((1, tm, d), row),
            pl.BlockSpec((1, 1, d), lambda bi, i: (bi, 0, gate_chunk)),
            *r_specs,
        ],
        out_specs=(pl.BlockSpec((1, tm, d), row), *r_out_specs),
        compiler_params=_cparams("arbitrary", "arbitrary"),
        name="conv_out",
    )(z, z, z, gb, conv_w, w_out, x, mod3, *r_in)
    return x1, routed


def _pack_rows(h, hp_ref):
    tm, d = h.shape
    bits = pltpu.bitcast(h.astype(BF16).astype(F32), jnp.uint32)
    words = (bits[:, :d // 2] >> 16) | (bits[:, d // 2:] & jnp.uint32(0xFFFF0000))
    for s in range(ROW_WORDS):
        hp_ref[pl.ds(s, tm, stride=ROW_WORDS), :] = words[:, s * LANES:(s + 1) * LANES]


def _unpack_rows(buf_ref, n_rows):
    lows, highs = [], []
    for s in range(ROW_WORDS):
        w = buf_ref[pl.ds(s, n_rows, stride=ROW_WORDS), :]
        lows.append(pltpu.bitcast(w << 16, F32).astype(BF16))
        highs.append(pltpu.bitcast(w & jnp.uint32(0xFFFF0000), F32).astype(BF16))
    return jnp.concatenate(lows + highs, axis=-1)


def _route_tile(x, g_ref, sh_ref, sc_ref, whi_ref, wlo_ref, br_ref, tri_ref, hp_ref, idx_ref,
                gate_ref, rank_ref, cnt_ref):
    h = _modulate(x, g_ref[...], sh_ref[0], sc_ref[0])
    _pack_rows(h, hp_ref)
    h_hi = h.astype(BF16)
    h_lo = (h - h_hi.astype(F32)).astype(BF16)
    logits = (jnp.dot(h_hi, whi_ref[...], preferred_element_type=F32)
              + (jnp.dot(h_lo, whi_ref[...], preferred_element_type=F32)
                 + jnp.dot(h_hi, wlo_ref[...], preferred_element_type=F32))) + br_ref[...]
    lane = lax.broadcasted_iota(jnp.int32, logits.shape, 1).astype(F32)
    work = jnp.where(lane < N_EXPERTS, logits, -jnp.inf)
    vals, idxs, picks = [], [], []
    for _ in range(TOP_K):
        mx = jnp.max(work, axis=-1, keepdims=True)
        ix = jnp.min(jnp.where(work == mx, lane, float(LANES)), axis=-1, keepdims=True)
        vals.append(mx)
        idxs.append(ix.astype(jnp.int32))
        picks.append(lane == ix)
        work = jnp.where(picks[-1], -jnp.inf, work)
    ex = [jnp.exp(v - vals[0]) for v in vals]
    den = ex[0] + ex[1] + ex[2] + ex[3]

    @pl.when(jnp.logical_and(pl.program_id(0) == 0, pl.program_id(1) == 0))
    def _():
        cnt_ref[...] = jnp.zeros_like(cnt_ref)

    picked = picks[0] | picks[1] | picks[2] | picks[3]
    tile_cnt = jnp.where(picked, 1.0, 0.0)
    before = jnp.dot(tri_ref[...], tile_cnt.astype(BF16),
                     preferred_element_type=F32) + cnt_ref[...]
    cnt_ref[...] = cnt_ref[...] + jnp.sum(tile_cnt, axis=0, keepdims=True)
    for k in range(TOP_K):
        idx_ref[:, k:k + 1] = idxs[k]
        gate_ref[:, k:k + 1] = ex[k] / den
        rank = jnp.sum(jnp.where(picks[k], before, 0.0), axis=-1, keepdims=True)
        rank_ref[:, k:k + 1] = rank.astype(jnp.int32)


N_ROUTE_IN = 7


def _route_specs(b, l, d, tm, norm_g, mod3, chunk0, w_r, b_r):
    nt = l // tm
    t = b * l
    w_hi = w_r.astype(BF16)
    w_lo = (w_r - w_hi.astype(F32)).astype(BF16)
    flat = lambda bi, i: (bi * nt + i, 0)
    const = lambda bi, i: (0, 0)
    earlier_tokens = jnp.tril(jnp.ones((tm, tm), BF16), -1)
    inputs = (norm_g, mod3, mod3, w_hi, w_lo, b_r, earlier_tokens)
    in_specs = [
        pl.BlockSpec((1, d), const),
        pl.BlockSpec((1, 1, d), lambda bi, i: (bi, 0, chunk0)),
        pl.BlockSpec((1, 1, d), lambda bi, i: (bi, 0, chunk0 + 1)),
        pl.BlockSpec((d, LANES), const),
        pl.BlockSpec((d, LANES), const),
        pl.BlockSpec((1, LANES), const),
        pl.BlockSpec((tm, tm), const),
    ]
    out_shapes = (jax.ShapeDtypeStruct((t * ROW_WORDS, LANES), jnp.uint32),
                  jax.ShapeDtypeStruct((t, TOP_K), jnp.int32),
                  jax.ShapeDtypeStruct((t, TOP_K), F32),
                  jax.ShapeDtypeStruct((t, TOP_K), jnp.int32),
                  jax.ShapeDtypeStruct((1, LANES), F32))
    out_specs = (pl.BlockSpec((tm * ROW_WORDS, LANES), flat),
                 pl.BlockSpec((tm, TOP_K), flat),
                 pl.BlockSpec((tm, TOP_K), flat),
                 pl.BlockSpec((tm, TOP_K), flat),
                 pl.BlockSpec((1, LANES), const))
    return inputs, in_specs, out_shapes, out_specs


def _issue_row_copies(n_rows, make_copy):
    def issue(it, carry):
        for u in range(DMA_UNROLL):
            make_copy(it * DMA_UNROLL + u).start(priority=u % 2)
        return carry

    lax.fori_loop(0, n_rows // DMA_UNROLL, issue, 0)


def _gather_kernel(nu_ref, tok_ref, tokn_ref, hp_ref, xs_ref, buf_ref, sem):
    i = pl.program_id(0)
    n_used = nu_ref[0]
    slot = lax.rem(i, 2)

    def issue_from(t_ref, dst_slot):
        def make_copy(r):
            src = pl.multiple_of(t_ref[0, 0, r] * ROW_WORDS, ROW_WORDS)
            return pltpu.make_async_copy(hp_ref.at[pl.ds(src, ROW_WORDS)],
                                         buf_ref.at[dst_slot, pl.ds(r * ROW_WORDS, ROW_WORDS)],
                                         sem.at[dst_slot])
        _issue_row_copies(GATHER_ROWS, make_copy)

    @pl.when(jnp.logical_and(i == 0, n_used > 0))
    def _():
        issue_from(tok_ref, 0)

    @pl.when(i + 1 < n_used)
    def _():
        issue_from(tokn_ref, 1 - slot)

    @pl.when(i < n_used)
    def _():
        pltpu.make_async_copy(hp_ref.at[pl.ds(0, GATHER_ROWS * ROW_WORDS)], buf_ref.at[slot],
                              sem.at[slot]).wait()
        xs_ref[...] = _unpack_rows(buf_ref.at[slot], GATHER_ROWS)

    @pl.when(i >= n_used)
    def _():
        xs_ref[...] = jnp.zeros_like(xs_ref)


def _gather_rows(n_used, slot_tok, hp, d):
    n_slots = slot_tok.shape[0]
    n_steps = n_slots // GATHER_ROWS
    tok3 = slot_tok.reshape(n_steps, 1, GATHER_ROWS)
    return pl.pallas_call(
        _gather_kernel,
        out_shape=jax.ShapeDtypeStruct((n_slots, d), BF16),
        grid_spec=pltpu.PrefetchScalarGridSpec(
            num_scalar_prefetch=1,
            grid=(n_steps,),
            in_specs=[
                pl.BlockSpec((1, 1, GATHER_ROWS), lambda i, nu: (i, 0, 0), memory_space=pltpu.SMEM),
                pl.BlockSpec((1, 1, GATHER_ROWS),
                             lambda i, nu: (jnp.minimum(i + 1, n_steps - 1), 0, 0),
                             memory_space=pltpu.SMEM),
                pl.BlockSpec(memory_space=pl.ANY),
            ],
            out_specs=pl.BlockSpec((GATHER_ROWS, d), lambda i, nu: (i, 0)),
            scratch_shapes=[pltpu.VMEM((2, GATHER_ROWS * ROW_WORDS, LANES), jnp.uint32),
                            pltpu.SemaphoreType.DMA((2,))],
        ),
        compiler_params=_cparams("arbitrary"),
        name="gather_rows",
    )(n_used, tok3, tok3, hp)


def _block_state(be_ref):
    bi = pl.program_id(1)
    used = bi < be_ref[pl.num_programs(1)]
    prev = be_ref[jnp.maximum(bi - 1, 0)]
    return used, jnp.logical_or(bi == 0, be_ref[bi] != prev)


def _gmm_gu_kernel(be_ref, x_ref, wg_ref, wl_ref, bg_ref, bl_ref, act_ref, wg_s, wl_s):
    used, changed = _block_state(be_ref)

    @pl.when(jnp.logical_and(used, changed))
    def _():
        wg_s[...] = wg_ref[...].astype(BF16)
        wl_s[...] = wl_ref[...].astype(BF16)

    @pl.when(used)
    def _():
        xb = x_ref[...]
        glu = jnp.dot(xb, wg_s[...], preferred_element_type=F32) + bg_ref[...]
        lin = jnp.dot(xb, wl_s[...], preferred_element_type=F32) + bl_ref[...]
        glu = jnp.minimum(glu, SWIGLU_LIMIT)
        lin = jnp.clip(lin, -SWIGLU_LIMIT, SWIGLU_LIMIT)
        sig = 1.0 / (1.0 + jnp.exp(-SWIGLU_ALPHA * glu))
        act_ref[...] = (glu * sig * (lin + 1.0)).astype(BF16)

    @pl.when(jnp.logical_not(used))
    def _():
        act_ref[...] = jnp.zeros_like(act_ref)


def _gmm_gu(block_e, xs, w_gu, b_gu4, layer):
    n_slots, d = xs.shape
    f = w_gu.shape[3] // 2
    tm, tn = EXPERT_TM, 1024
    nn = f // tn
    return pl.pallas_call(
        _gmm_gu_kernel,
        out_shape=jax.ShapeDtypeStruct((n_slots, f), BF16),
        grid_spec=pltpu.PrefetchScalarGridSpec(
            num_scalar_prefetch=1,
            grid=(nn, n_slots // tm),
            in_specs=[
                pl.BlockSpec((tm, d), lambda j, i, be: (i, 0)),
                pl.BlockSpec((None, None, d, tn), lambda j, i, be: (layer, be[i], 0, j)),
                pl.BlockSpec((None, None, d, tn), lambda j, i, be: (layer, be[i], 0, nn + j)),
                pl.BlockSpec((None, None, 1, tn), lambda j, i, be: (layer, be[i], 0, j)),
                pl.BlockSpec((None, None, 1, tn), lambda j, i, be: (layer, be[i], 0, nn + j)),
            ],
            out_specs=pl.BlockSpec((tm, tn), lambda j, i, be: (i, j)),
            scratch_shapes=[pltpu.VMEM((d, tn), BF16), pltpu.VMEM((d, tn), BF16)],
        ),
        compiler_params=_cparams("arbitrary", "arbitrary"),
        name="gmm_gate_up",
    )(block_e, xs, w_gu, w_gu, b_gu4, b_gu4)


DOWN_TN = SUBLANES * LANES


def _gmm_down_kernel(be_ref, a_ref, w_ref, b_ref, y_ref, w_s):
    used, changed = _block_state(be_ref)

    @pl.when(jnp.logical_and(used, changed))
    def _():
        w_s[...] = w_ref[...].astype(BF16)

    @pl.when(used)
    def _():
        y = jnp.dot(a_ref[...], w_s[...], preferred_element_type=F32) + b_ref[...]
        tm = y.shape[0]
        y2_ref = y_ref.reshape(tm * SUBLANES, LANES)
        for cc in range(SUBLANES):
            y2_ref[pl.ds(cc, tm, stride=SUBLANES), :] = y[:, cc * LANES:(cc + 1) * LANES]

    @pl.when(jnp.logical_not(used))
    def _():
        y_ref[...] = jnp.zeros_like(y_ref)


def _gmm_down(block_e, act, w_down, b_down4, layer):
    n_slots, f = act.shape
    d = w_down.shape[3]
    tm, tn = EXPERT_TM, DOWN_TN
    return pl.pallas_call(
        _gmm_down_kernel,
        out_shape=jax.ShapeDtypeStruct((n_slots, d // LANES, LANES), F32),
        grid_spec=pltpu.PrefetchScalarGridSpec(
            num_scalar_prefetch=1,
            grid=(d // tn, n_slots // tm),
            in_specs=[
                pl.BlockSpec((tm, f), lambda j, i, be: (i, 0)),
                pl.BlockSpec((None, None, f, tn), lambda j, i, be: (layer, be[i], 0, j)),
                pl.BlockSpec((None, None, 1, tn), lambda j, i, be: (layer, be[i], 0, j)),
            ],
            out_specs=pl.BlockSpec((tm, SUBLANES, LANES), lambda j, i, be: (i, j, 0)),
            scratch_shapes=[pltpu.VMEM((f, tn), BF16)],
        ),
        compiler_params=_cparams("arbitrary", "arbitrary"),
        name="gmm_down",
    )(block_e, act, w_down, b_down4)


COMBINE_ROWS = COMBINE_TOK * TOP_K
COMBINE_SUB = 32


def _combine_kernel(pos_ref, posn_ref, ys_ref, gate_ref, x_ref, g2_ref, gf_ref, out_ref,
                    buf_ref, sem, *, final_norm):
    i = pl.program_id(0)
    n = pl.num_programs(0)
    slot = lax.rem(i, 2)
    n_chunks = x_ref.shape[1] // LANES

    def issue_from(p_ref, dst_slot):
        def make_copy(r):
            src = pl.multiple_of(p_ref[0, 0, r] * n_chunks, n_chunks)
            return pltpu.make_async_copy(ys_ref.at[pl.ds(src, n_chunks)],
                                         buf_ref.at[dst_slot, pl.ds(r * n_chunks, n_chunks)],
                                         sem.at[dst_slot])
        _issue_row_copies(COMBINE_ROWS, make_copy)

    @pl.when(i == 0)
    def _():
        issue_from(pos_ref, 0)

    @pl.when(i + 1 < n)
    def _():
        issue_from(posn_ref, 1 - slot)

    pltpu.make_async_copy(ys_ref.at[pl.ds(0, COMBINE_ROWS * n_chunks)], buf_ref.at[slot],
                          sem.at[slot]).wait()

    rows_ref = buf_ref.at[slot]
    d = n_chunks * LANES
    for t0 in range(0, COMBINE_TOK, COMBINE_SUB):
        tok = slice(t0, t0 + COMBINE_SUB)
        gates = [jnp.broadcast_to(gate_ref[tok, k:k + 1], (COMBINE_SUB, LANES)) for k in range(TOP_K)]
        ssq = jnp.zeros((COMBINE_SUB, LANES), F32)
        for c in range(n_chunks):
            cols = slice(c * LANES, (c + 1) * LANES)
            y = None
            for k in range(TOP_K):
                first = (k * COMBINE_TOK + t0) * n_chunks + c
                rows = rows_ref[pl.ds(first, COMBINE_SUB, stride=n_chunks), :]
                y = gates[k] * rows if y is None else y + gates[k] * rows
            o = x_ref[tok, cols] + g2_ref[0, :, cols] * y
            out_ref[tok, cols] = o
            ssq = ssq + o * o
        if final_norm:
            inv = lax.rsqrt(jnp.sum(ssq, axis=-1, keepdims=True) / d + RMS_EPS)
            out_ref[tok, :] = out_ref[tok, :] * inv * gf_ref[...]


def _combine(pos, ys2, gate, x2, mod3, gate_chunk, tokens_per_batch, final_g, final_norm):
    t, d = x2.shape
    n_chunks = d // LANES
    n_steps = t // COMBINE_TOK
    steps_per_batch = tokens_per_batch // COMBINE_TOK
    pos3 = pos.reshape(n_steps, COMBINE_TOK, TOP_K).transpose(0, 2, 1).reshape(n_steps, 1, COMBINE_ROWS)
    return pl.pallas_call(
        functools.partial(_combine_kernel, final_norm=final_norm),
        out_shape=jax.ShapeDtypeStruct((t, d), F32),
        grid=(n_steps,),
        in_specs=[
            pl.BlockSpec((1, 1, COMBINE_ROWS), lambda i: (i, 0, 0), memory_space=pltpu.SMEM),
            pl.BlockSpec((1, 1, COMBINE_ROWS), lambda i: (jnp.minimum(i + 1, n_steps - 1), 0, 0),
                         memory_space=pltpu.SMEM),
            pl.BlockSpec(memory_space=pl.ANY),
            pl.BlockSpec((COMBINE_TOK, TOP_K), lambda i: (i, 0)),
            pl.BlockSpec((COMBINE_TOK, d), lambda i: (i, 0)),
            pl.BlockSpec((1, 1, d), lambda i: (i // steps_per_batch, 0, gate_chunk)),
            pl.BlockSpec((1, d), lambda i: (0, 0)),
        ],
        out_specs=pl.BlockSpec((COMBINE_TOK, d), lambda i: (i, 0)),
        scratch_shapes=[pltpu.VMEM((2, COMBINE_ROWS * n_chunks, LANES), F32),
                        pltpu.SemaphoreType.DMA((2,))],
        compiler_params=_cparams("arbitrary"),
        name="combine",
    )(pos3, pos3, ys2, gate, x2, mod3, final_g)


SCATTER_CHUNK = 8192
TOP_K_SHIFT = 2
assert 1 << TOP_K_SHIFT == TOP_K and DMA_UNROLL % TOP_K == 0


def _slot_tok_kernel(pos_ref, zeros_ref, out_ref):
    i = pl.program_id(0)

    @pl.when(i == 0)
    def _():
        pltpu.sync_copy(zeros_ref, out_ref)

    base = i * SCATTER_CHUNK

    def scatter(it, carry):
        a0 = it * DMA_UNROLL
        tok0 = lax.shift_right_logical(base + a0, TOP_K_SHIFT)
        for u in range(DMA_UNROLL):
            out_ref[pos_ref[0, 0, a0 + u]] = tok0 + (u >> TOP_K_SHIFT)
        return carry

    lax.fori_loop(0, SCATTER_CHUNK // DMA_UNROLL, scatter, 0)


def _slot_tokens(pos, n_slots):
    n_steps = pos.shape[0] // SCATTER_CHUNK
    return pl.pallas_call(
        _slot_tok_kernel,
        out_shape=jax.ShapeDtypeStruct((n_slots,), jnp.int32),
        grid=(n_steps,),
        in_specs=[pl.BlockSpec((1, 1, SCATTER_CHUNK), lambda i: (i, 0, 0),
                               memory_space=pltpu.SMEM),
                  pl.BlockSpec(memory_space=pl.ANY)],
        out_specs=pl.BlockSpec(memory_space=pltpu.SMEM),
        compiler_params=_cparams("arbitrary"),
        name="slot_tokens",
    )(pos.reshape(n_steps, 1, SCATTER_CHUNK), jnp.zeros((n_slots,), jnp.int32))


def _slot_plan(top_idx, rank, expert_counts):
    t = top_idx.shape[0]
    n_assign = t * TOP_K
    flat_e = top_idx.reshape(-1)
    counts = expert_counts[0, :N_EXPERTS].astype(jnp.int32)
    rank = rank.reshape(-1)
    padded = (counts + EXPERT_TM - 1) // EXPERT_TM * EXPERT_TM
    padded_end = jnp.cumsum(padded)
    padded_start = padded_end - padded
    pos = (padded_start[flat_e] + rank).astype(jnp.int32)
    n_blocks = -(-(n_assign + N_EXPERTS * (EXPERT_TM - 1)) // EXPERT_TM)
    n_blocks = -(-n_blocks * EXPERT_TM // GATHER_ROWS) * GATHER_ROWS // EXPERT_TM
    n_slots = n_blocks * EXPERT_TM
    slot_tok = _slot_tokens(pos, n_slots)
    block_start = jnp.arange(n_blocks, dtype=jnp.int32) * EXPERT_TM
    block_e = jnp.minimum(jnp.sum(block_start[:, None] >= padded_end[None, :], axis=1),
                          N_EXPERTS - 1).astype(jnp.int32)
    n_used = (padded_end[-1:] // EXPERT_TM).astype(jnp.int32)
    return pos, slot_tok, jnp.concatenate([block_e, n_used]), n_used


def _expert_ffn_residual(x, routed, mod3, layer, w_gu, b_gu4, w_down, b_down4, final_g,
                         final_norm):
    b, l, d = x.shape
    assert d == 2 * ROW_WORDS * LANES
    hp, top_idx, gate, rank, expert_counts = routed
    pos, slot_tok, block_e, n_used = _slot_plan(top_idx, rank, expert_counts)
    xs = _gather_rows(n_used, slot_tok, hp, d)
    act = _gmm_gu(block_e, xs, w_gu, b_gu4, layer)
    ys3 = _gmm_down(block_e, act, w_down, b_down4, layer)
    ys2 = ys3.reshape(ys3.shape[0] * ys3.shape[1], LANES)
    out = _combine(pos, ys2, gate, x.reshape(b * l, d), mod3, 5, l, final_g, final_norm)
    return out.reshape(b, l, d)


def _rope_tables(length):
    rows = length // GRID_W
    row = jnp.repeat(jnp.arange(rows), GRID_W).astype(F32)
    col = jnp.tile(jnp.arange(GRID_W), rows).astype(F32)
    inv = 1.0 / (ROPE_THETA ** (jnp.arange(0, ROPE_AXIS, 2, dtype=F32) / ROPE_AXIS))
    ang_r = row[:, None] * inv[None, :]
    ang_c = col[:, None] * inv[None, :]
    cos = jnp.concatenate([jnp.cos(ang_r)] * 2 + [jnp.cos(ang_c)] * 2, axis=-1)
    sin = jnp.concatenate([jnp.sin(ang_r)] * 2 + [jnp.sin(ang_c)] * 2, axis=-1)
    return cos, sin


def _rotate_half_cols(w):
    a, b_, c_, d_ = jnp.split(w, 4, axis=-1)
    return jnp.concatenate([-b_, a, -d_, c_], axis=-1)


def kernel(x, c, ctx, c_ctx, ada_w, ada_b, norm_mix_g, norm_ffn_g, mla_w_in, mla_q_norm_g,
           mla_kv_norm_g, mla_w_q_up, mla_w_kv_up, mla_w_out, conv_w_in, conv_w, conv_w_out,
           router_w, router_b, expert_w_gu, expert_b_gu, expert_w_down, expert_b_down,
           final_norm_g):
    b, l, d = x.shape
    depth = ada_w.shape[0]
    lc = ctx.shape[1]
    assert depth == 2 and b + 1 <= ADA_ROWS

    cv = jnp.concatenate([c, c_ctx[None, :], jnp.zeros((ADA_ROWS - b - 1, d), F32)], axis=0)
    ctx_row = b
    ada_b3 = ada_b.reshape(depth, 1, N_ADA * d)
    cos, sin = _rope_tables(l)
    b_gu4 = expert_b_gu.reshape(depth, N_EXPERTS, 1, -1)
    b_down4 = expert_b_down.reshape(depth, N_EXPERTS, 1, d)
    w_r = jnp.pad(router_w, ((0, 0), (0, 0), (0, LANES - N_EXPERTS)))
    b_r = jnp.pad(router_b, ((0, 0), (0, LANES - N_EXPERTS)))[:, None, :]
    final_g = final_norm_g[None, :]

    mod3 = _ada(cv, ada_w, ada_b3, 0).reshape(ADA_ROWS, 1, N_ADA * d)
    w_in = mla_w_in[0]
    w_kr = w_in[:, Q_LORA + KV_LORA:]
    w_all = jnp.concatenate([w_in, _rotate_half_cols(w_kr)], axis=1).astype(BF16)
    w_ctx = w_in[:, Q_LORA:].astype(BF16)
    gq = mla_q_norm_g[0][None, :]
    gkv = mla_kv_norm_g[0][None, :]
    g_mix = norm_mix_g[0][None, :]
    wq = mla_w_q_up[0].reshape(Q_LORA, N_HEADS, QK_DIM)
    wq_rope = wq[..., QK_NOPE:]
    wq_all = jnp.concatenate([wq, _rotate_half_cols(wq_rope)], axis=-1)
    wq_all = wq_all.reshape(Q_LORA, N_HEADS * Q_HEAD_COLS).astype(BF16)
    wkv = mla_w_kv_up[0].astype(BF16)
    w_o = mla_w_out[0].astype(BF16)

    cq, ckv, kr = _mla_in(x, g_mix, mod3, 0, w_all, gq, gkv, cos, sin)
    ckv_c, kr_c = _mla_in_ctx(ctx, g_mix, mod3, ctx_row, w_ctx, gkv)
    ckv_all = jnp.concatenate([ckv_c, ckv], axis=1)
    kr_all = jnp.concatenate([kr_c, kr], axis=1)
    q = _q_up(cq, wq_all, cos, sin)
    k, v = _kv_up(ckv_all, kr_all, wkv, 768)
    o = _attention(q, k, v, 1024, 768)
    x, routed = _proj_res(o, w_o, x, mod3, 2, norm_ffn_g[0][None, :], w_r[0], b_r[0])
    x = _expert_ffn_residual(x, routed, mod3, 0, expert_w_gu, b_gu4, expert_w_down, b_down4,
                             final_g, False)

    mod3 = _ada(cv, ada_w, ada_b3, 1).reshape(ADA_ROWS, 1, N_ADA * d)
    gb, z = _conv_in(x, norm_mix_g[1][None, :], mod3, 0, conv_w_in[0].astype(BF16))
    x, routed = _conv_out(z, gb, conv_w[0], conv_w_out[0].astype(BF16), x, mod3, 2,
                          norm_ffn_g[1][None, :], w_r[1], b_r[1])
    x = _expert_ffn_residual(x, routed, mod3, 1, expert_w_gu, b_gu4, expert_w_down, b_down4,
                             final_g, True)
    return x
```
